```python
import math
import jax, jax.numpy as jnp
from jax import lax
import numpy as np

D_MODEL = 1024
BATCH = 32
SEQ = 256
DEPTH = 4
DEC_BATCH = 8
DEC_SEQ = 4096
PAST_LEN = 256

GRID_W = 64
N_MIXERS = 2
N_HEADS = 16
HEAD_DIM = D_MODEL // N_HEADS
WIN_ROWS_MAX = 8
WIN_COLS = 16
N_ATTN_LAYERS = (DEPTH + 1) // 2
N_POOL_LAYERS = DEPTH // 2
POOL_SIZES = (2, 4, 8, 16)
N_POOL_GROUPS = 4
POOL_GROUP_DIM = D_MODEL // N_POOL_GROUPS
N_EXPERT_GROUPS = 4
EXPERTS_PER_GROUP = 4
N_EXPERTS = N_EXPERT_GROUPS * EXPERTS_PER_GROUP
TOP_K_FINE = 2
D_FF_EXPERT = D_MODEL // 2
EPS = 1e-6

kernel_name = 'hybrid_natten_pool_hmoe_diffusion_step'


def rmsnorm(x, g):
    xf = x.astype(jnp.float32)
    y = xf * lax.rsqrt(jnp.mean(xf * xf, axis=-1, keepdims=True) + EPS)
    return (y * g.astype(jnp.float32)).astype(x.dtype)


def ada_modulation(cond, w_mod, b_mod):
    m = jax.nn.silu(cond) @ w_mod + b_mod
    return [t[:, None, :] for t in jnp.split(m, 6, axis=-1)]


def modulate(h, shift, scale):
    return h * (1 + scale) + shift


def project_qkv(h, w_qkv, q_g, k_g):
    b, t, _ = h.shape
    qkv = (h @ w_qkv).reshape(b, t, 3, N_HEADS, HEAD_DIM)
    q = rmsnorm(qkv[:, :, 0], q_g)
    k = rmsnorm(qkv[:, :, 1], k_g)
    v = qkv[:, :, 2]
    return q, k, v


def context_attention(q, k, v):
    s = jnp.einsum('bqhd,bkhd->bhqk', q, k).astype(jnp.float32) * HEAD_DIM ** -0.5
    p = jax.nn.softmax(s, axis=-1).astype(v.dtype)
    return jnp.einsum('bhqk,bkhd->bqhd', p, v)


def neighbourhood_attention(q, k, v, k_ctx, v_ctx, rpb):
    b, t, h, dh = q.shape
    rows = t // GRID_W
    wr = min(WIN_ROWS_MAX, rows)
    wc = WIN_COLS
    qg = q.reshape(b, rows, GRID_W, h, dh)
    kg = k.reshape(b, rows, GRID_W, h, dh)
    vg = v.reshape(b, rows, GRID_W, h, dh)
    cols = jnp.arange(GRID_W)
    col_start = jnp.clip(cols - wc // 2, 0, GRID_W - wc)
    col_idx = col_start[:, None] + jnp.arange(wc)[None, :]
    col_off = col_idx - cols[:, None] + (WIN_COLS - 1)
    rpb_cols = rpb[:, :, col_off]
    n_loc = wr * wc
    scale = HEAD_DIM ** -0.5

    def one_row(args):
        r, q_r = args
        rs = jnp.clip(r - wr // 2, 0, rows - wr)
        k_band = lax.dynamic_slice_in_dim(kg, rs, wr, axis=1)
        v_band = lax.dynamic_slice_in_dim(vg, rs, wr, axis=1)
        k_win = k_band[:, :, col_idx]
        v_win = v_band[:, :, col_idx]
        row_off = rs + jnp.arange(wr) - r + (WIN_ROWS_MAX - 1)
        bias = jnp.take(rpb_cols, row_off, axis=1).transpose(0, 2, 1, 3)
        s_loc = (jnp.einsum('bqhd,brqkhd->bhqrk', q_r, k_win).astype(jnp.float32) * scale
                 + bias[None].astype(jnp.float32))
        s_ctx = jnp.einsum('bqhd,bchd->bhqc', q_r, k_ctx).astype(jnp.float32) * scale
        s = jnp.concatenate([s_loc.reshape(b, h, GRID_W, n_loc), s_ctx], axis=-1)
        p = jax.nn.softmax(s, axis=-1).astype(q.dtype)
        p_loc = p[..., :n_loc].reshape(b, h, GRID_W, wr, wc)
        p_ctx = p[..., n_loc:]
        return (jnp.einsum('bhqrk,brqkhd->bqhd', p_loc, v_win)
                + jnp.einsum('bhqc,bchd->bqhd', p_ctx, v_ctx))

    out = lax.map(one_row, (jnp.arange(rows), jnp.moveaxis(qg, 1, 0)))
    return jnp.moveaxis(out, 0, 1).reshape(b, t, h, dh)


def multiscale_pool(h, w_pool, pool_scale):
    b, t, _ = h.shape
    hf = h.astype(jnp.float32).reshape(b, t, N_POOL_GROUPS, POOL_GROUP_DIM)
    csum = jnp.concatenate([jnp.zeros((b, 1, N_POOL_GROUPS, POOL_GROUP_DIM), jnp.float32),
                            jnp.cumsum(hf, axis=1)], axis=1)
    pos = jnp.arange(t)
    pooled = []
    for g, w in enumerate(POOL_SIZES):
        lo = jnp.clip(pos - w // 2, 0, t)
        hi = jnp.clip(pos - w // 2 + w, 0, t)
        cg = csum[:, :, g]
        pooled.append((cg[:, hi] - cg[:, lo]) / (hi - lo).astype(jnp.float32)[None, :, None])
    pooled = jnp.stack(pooled, axis=2)
    diff = (pooled - hf).astype(h.dtype)
    y = jnp.einsum('btgc,gcd->btgd', diff, w_pool).reshape(b, t, D_MODEL)
    return y * pool_scale


def hierarchical_moe(h, w_rg, b_rg, w_re, b_re, w_gate, w_up, w_down):
    shape = h.shape
    x = h.reshape(-1, D_MODEL)
    n = x.shape[0]
    xf = x.astype(jnp.float32)
    g_logits = xf @ w_rg.astype(jnp.float32) + b_rg.astype(jnp.float32)
    g_sel = jnp.argmax(g_logits, axis=-1)
    g_w = jnp.take_along_axis(jax.nn.softmax(g_logits, axis=-1), g_sel[:, None], axis=1)
    e_logits = (xf @ w_re.astype(jnp.float32) + b_re.astype(jnp.float32)).reshape(
        n, N_EXPERT_GROUPS, EXPERTS_PER_GROUP)
    e_in = jnp.take_along_axis(e_logits, g_sel[:, None, None], axis=1)[:, 0]
    top_v, top_i = lax.top_k(e_in, TOP_K_FINE)
    w_k = g_w * jax.nn.softmax(top_v, axis=-1)
    eid = g_sel[:, None] * EXPERTS_PER_GROUP + top_i
    combine = jnp.einsum('nk,nke->ne', w_k, jax.nn.one_hot(eid, N_EXPERTS, dtype=jnp.float32))
    out = jnp.zeros((n, D_MODEL), jnp.float32)
    for grp in range(N_EXPERT_GROUPS):
        sl = slice(grp * EXPERTS_PER_GROUP, (grp + 1) * EXPERTS_PER_GROUP)
        a = jnp.einsum('nd,edf->nef', x, w_gate[sl])
        u = jnp.einsum('nd,edf->nef', x, w_up[sl])
        act = jax.nn.silu(a) * u * combine[:, sl, None].astype(x.dtype)
        out = out + jnp.einsum('nef,efd->nd', act, w_down[sl]).astype(jnp.float32)
    return out.astype(h.dtype).reshape(shape)


def setup_inputs(seed: int = 0) -> dict:
    key = jax.random.key(seed)
    ks = jax.random.split(key, 24)
    D = D_MODEL

    def nrm(k, shape, s):
        return jax.random.normal(k, shape, jnp.float32) * s

    return {
        'x_prompt': nrm(ks[0], (BATCH, SEQ, D), 1.0),
        'x_sample': nrm(ks[1], (DEC_BATCH, DEC_SEQ, D), 1.0),
        'cache_k': nrm(ks[2], (DEC_BATCH, N_ATTN_LAYERS, PAST_LEN, N_HEADS, HEAD_DIM), 1.0),
        'cache_v': nrm(ks[3], (DEC_BATCH, N_ATTN_LAYERS, PAST_LEN, N_HEADS, HEAD_DIM), 1.0),
        'c': nrm(ks[4], (DEC_BATCH, D), 1.0),
        'c_ctx': nrm(ks[5], (D,), 1.0),
        'norm1_g': 1.0 + nrm(ks[6], (DEPTH, D), 0.05),
        'norm2_g': 1.0 + nrm(ks[7], (DEPTH, D), 0.05),
        'w_mod': nrm(ks[8], (DEPTH, D, 6 * D), 0.5 * D ** -0.5),
        'b_mod': nrm(ks[9], (DEPTH, 6 * D), 0.01),
        'w_qkv': nrm(ks[10], (N_ATTN_LAYERS, D, 3 * D), D ** -0.5),
        'q_norm_g': 1.0 + nrm(ks[11], (N_ATTN_LAYERS, HEAD_DIM), 0.05),
        'k_norm_g': 1.0 + nrm(ks[12], (N_ATTN_LAYERS, HEAD_DIM), 0.05),
        'rpb': nrm(ks[13], (N_ATTN_LAYERS, N_HEADS, 2 * WIN_ROWS_MAX - 1, 2 * WIN_COLS - 1), 0.1),
        'w_o': nrm(ks[14], (N_ATTN_LAYERS, D, D), D ** -0.5),
        'w_pool': nrm(ks[15], (N_POOL_LAYERS, N_POOL_GROUPS, POOL_GROUP_DIM, POOL_GROUP_DIM),
                      POOL_GROUP_DIM ** -0.5),
        'pool_scale': 1.0 + nrm(ks[16], (N_POOL_LAYERS, D), 0.05),
        'w_router_group': nrm(ks[17], (DEPTH, D, N_EXPERT_GROUPS), D ** -0.5),
        'b_router_group': nrm(ks[18], (DEPTH, N_EXPERT_GROUPS), 0.01),
        'w_router_expert': nrm(ks[19], (DEPTH, D, N_EXPERTS), D ** -0.5),
        'b_router_expert': nrm(ks[20], (DEPTH, N_EXPERTS), 0.01),
        'w_gate': nrm(ks[21], (DEPTH, N_EXPERTS, D, D_FF_EXPERT), D ** -0.5),
        'w_up': nrm(ks[22], (DEPTH, N_EXPERTS, D, D_FF_EXPERT), D ** -0.5),
        'w_down': nrm(ks[23], (DEPTH, N_EXPERTS, D_FF_EXPERT, D), D_FF_EXPERT ** -0.5),
    }


def reference(x_prompt, x_sample, cache_k, cache_v, c, c_ctx, norm1_g, norm2_g, w_mod, b_mod,
              w_qkv, q_norm_g, k_norm_g, rpb, w_o, w_pool, pool_scale,
              w_router_group, b_router_group, w_router_expert, b_router_expert,
              w_gate, w_up, w_down):
    xp = x_prompt
    xs = x_sample
    bp, lp, _ = xp.shape
    bs, ls, _ = xs.shape
    new_k, new_v = [], []
    for l in range(DEPTH):
        mp = ada_modulation(c_ctx[None, :], w_mod[l], b_mod[l])
        ms = ada_modulation(c, w_mod[l], b_mod[l])
        hp = modulate(rmsnorm(xp, norm1_g[l]), mp[0], mp[1])
        hs = modulate(rmsnorm(xs, norm1_g[l]), ms[0], ms[1])
        j = l // N_MIXERS
        if l % N_MIXERS == 0:
            qp, kp, vp = project_qkv(hp, w_qkv[j], q_norm_g[j], k_norm_g[j])
            new_k.append(kp)
            new_v.append(vp)
            op = context_attention(qp, kp, vp)
            qs, ks_, vs = project_qkv(hs, w_qkv[j], q_norm_g[j], k_norm_g[j])
            os_ = neighbourhood_attention(qs, ks_, vs, cache_k[:, j], cache_v[:, j], rpb[j])
            yp = op.reshape(bp, lp, D_MODEL) @ w_o[j]
            ys = os_.reshape(bs, ls, D_MODEL) @ w_o[j]
        else:
            yp = multiscale_pool(hp, w_pool[j], pool_scale[j])
            ys = multiscale_pool(hs, w_pool[j], pool_scale[j])
        xp = xp + mp[2] * yp
        xs = xs + ms[2] * ys
        hp = modulate(rmsnorm(xp, norm2_g[l]), mp[3], mp[4])
        hs = modulate(rmsnorm(xs, norm2_g[l]), ms[3], ms[4])
        xp = xp + mp[5] * hierarchical_moe(hp, w_router_group[l], b_router_group[l],
                                           w_router_expert[l], b_router_expert[l],
                                           w_gate[l], w_up[l], w_down[l])
        xs = xs + ms[5] * hierarchical_moe(hs, w_router_group[l], b_router_group[l],
                                           w_router_expert[l], b_router_expert[l],
                                           w_gate[l], w_up[l], w_down[l])
    y_prompt = xp
    y_sample = xs
    new_cache_k = jnp.stack(new_k, axis=1)
    new_cache_v = jnp.stack(new_v, axis=1)
    return (y_prompt, y_sample, new_cache_k, new_cache_v)
```

```python
import functools

import jax
import jax.numpy as jnp
from jax import lax
from jax.experimental import pallas as pl
from jax.experimental.pallas import tpu as pltpu

D_MODEL = 1024
N_HEADS = 16
HEAD_DIM = D_MODEL // N_HEADS
GRID_W = 64
WIN_ROWS = 8
WIN_COLS = 16
POOL_SIZES = (2, 4, 8, 16)
POOL_GROUP_DIM = D_MODEL // len(POOL_SIZES)
POOL_HALO = 8
N_EXPERT_GROUPS = 4
EXPERTS_PER_GROUP = 4
N_EXPERTS = N_EXPERT_GROUPS * EXPERTS_PER_GROUP
N_PAIRS = 6
N_BUCKETS = N_EXPERT_GROUPS * N_PAIRS
D_FF = D_MODEL // 2
EPS = 1e-6
NEG = -1e30

LANES = 128
MXU_DIM = 256
TOKEN_TILE = 512
MOE_TILE = 256
ROUTER_ROWS = 32
COND_ROWS = 16
VMEM_LIMIT = 56 * 1024 * 1024

BF16 = jnp.bfloat16
F32 = jnp.float32


def _params(n_grid_dims, vmem=VMEM_LIMIT):
    return pltpu.CompilerParams(
        dimension_semantics=("arbitrary",) * n_grid_dims, vmem_limit_bytes=vmem)


def _rms_modulate(x, g, shift, scale):
    y = x * lax.rsqrt(jnp.mean(x * x, axis=-1, keepdims=True) + EPS)
    return (y * g) * (1.0 + scale) + shift


def _mod_kernel(cond_ref, w_ref, b_ref, o_ref):
    c = cond_ref[...]
    s = c / (1.0 + jnp.exp(-c))
    o_ref[...] = jnp.dot(s, w_ref[...], preferred_element_type=F32,
                         precision=lax.Precision.HIGHEST) + b_ref[...]


def _modulation(cond, w_mod, b_mod):
    depth = w_mod.shape[0]
    tn = 1536
    out = pl.pallas_call(
        _mod_kernel,
        grid=(depth, 6 * D_MODEL // tn),
        in_specs=[
            pl.BlockSpec((COND_ROWS, D_MODEL), lambda l, n: (0, 0)),
            pl.BlockSpec((None, D_MODEL, tn), lambda l, n: (l, 0, n)),
            pl.BlockSpec((None, 1, tn), lambda l, n: (l, 0, n)),
        ],
        out_specs=pl.BlockSpec((None, COND_ROWS, tn), lambda l, n: (l, 0, n)),
        out_shape=jax.ShapeDtypeStruct((depth, COND_ROWS, 6 * D_MODEL), F32),
        compiler_params=_params(2),
        name="adaln_modulation",
    )(cond, w_mod, b_mod.reshape(depth, 1, 6 * D_MODEL))
    return out.reshape(depth, COND_ROWS, 6, D_MODEL)


def _mod_spec(layer, tiles_per_cond_row):
    if tiles_per_cond_row is None:
        return pl.BlockSpec((None, None, 6, D_MODEL), lambda i, *_: (layer, 0, 0, 0))
    return pl.BlockSpec((None, None, 6, D_MODEL),
                        lambda i, *_: (layer, 1 + i // tiles_per_cond_row, 0, 0))


def _head_rms(t, ones_bd, gain):
    sq = (t * t).astype(BF16)
    parts = [jnp.dot(sq[:, c:c + MXU_DIM], ones_bd, preferred_element_type=F32)
             for c in range(0, D_MODEL, MXU_DIM)]
    mean = jnp.concatenate(parts, axis=-1) / HEAD_DIM
    return t * lax.rsqrt(mean + EPS) * gain


def _qkv_kernel(x_ref, mod_ref, g_ref, w_ref, qg_ref, kg_ref, bd_ref, q_ref, k_ref, v_ref):
    m = mod_ref[...]
    h = _rms_modulate(x_ref[...], g_ref[...], m[0:1], m[1:2]).astype(BF16)
    qkv = jnp.dot(h, w_ref[...], preferred_element_type=F32)
    bd = bd_ref[...]
    q = _head_rms(qkv[:, :D_MODEL], bd, qg_ref[...])
    k = _head_rms(qkv[:, D_MODEL:2 * D_MODEL], bd, kg_ref[...])
    q_ref[...] = (q * HEAD_DIM ** -0.5).astype(q_ref.dtype)
    k_ref[...] = k.astype(k_ref.dtype)
    v_ref[...] = qkv[:, 2 * D_MODEL:].astype(v_ref.dtype)


def _qkv(x, mods, layer, tiles_per_cond_row, g, w_bf16, qg, kg, bd, kv_dtype):
    n = x.shape[0]
    row = lambda i: (i, 0)
    const = lambda i: (0, 0)
    tok = pl.BlockSpec((TOKEN_TILE, D_MODEL), row)
    vec = pl.BlockSpec((1, D_MODEL), const)
    return pl.pallas_call(
        _qkv_kernel,
        grid=(n // TOKEN_TILE,),
        in_specs=[tok, _mod_spec(layer, tiles_per_cond_row), vec,
                  pl.BlockSpec((D_MODEL, 3 * D_MODEL), const), vec, vec,
                  pl.BlockSpec((MXU_DIM, MXU_DIM), const)],
        out_specs=[tok, tok, tok],
        out_shape=[jax.ShapeDtypeStruct((n, D_MODEL), BF16),
                   jax.ShapeDtypeStruct((n, D_MODEL), kv_dtype),
                   jax.ShapeDtypeStruct((n, D_MODEL), kv_dtype)],
        compiler_params=_params(1),
        name="qkv_proj",
    )(x, mods, g, w_bf16, qg, kg, bd)


def _dot_nt(a, b):
    return lax.dot_general(a, b, (((1,), (1,)), ((), ())), preferred_element_type=F32)


def _pair_attention(q2, score_fn, value_fn):
    lane = lax.broadcasted_iota(jnp.int32, q2.shape, 1)
    first = lane < HEAD_DIM
    outs = []
    for keep in (first, jnp.logical_not(first)):
        s = score_fn(jnp.where(keep, q2, jnp.zeros_like(q2)))
        m = functools.reduce(jnp.maximum, [jnp.max(b, axis=-1, keepdims=True) for b in s])
        p = [jnp.exp(b - m) for b in s]
        l = functools.reduce(jnp.add, [jnp.sum(b, axis=-1, keepdims=True) for b in p])
        outs.append(value_fn([b.astype(BF16) for b in p]) / l)
    return jnp.where(first, outs[0], outs[1])


def _ctx_attn_kernel(x_ref, mod_ref, q_ref, k_ref, v_ref, wo_ref, o_ref, att_ref):
    for hp in range(N_HEADS // 2):
        cols = slice(hp * LANES, (hp + 1) * LANES)
        k2 = k_ref[:, cols].astype(BF16)
        v2 = v_ref[:, cols].astype(BF16)
        att = _pair_attention(
            q_ref[:, cols],
            lambda q: [_dot_nt(q, k2)],
            lambda p: jnp.dot(p[0], v2, preferred_element_type=F32))
        att_ref[:, cols] = att.astype(BF16)
    y = jnp.dot(att_ref[...], wo_ref[...], preferred_element_type=F32)
    o_ref[...] = x_ref[...] + mod_ref[...][2:3] * y


def _ctx_attention(x, mods, layer, q, k, v, wo_bf16, seq):
    n = x.shape[0]
    row = lambda i: (i, 0)
    tok = pl.BlockSpec((seq, D_MODEL), row)
    return pl.pallas_call(
        _ctx_attn_kernel,
        grid=(n // seq,),
        in_specs=[tok, _mod_spec(layer, None), tok, tok, tok,
                  pl.BlockSpec((D_MODEL, D_MODEL), lambda i: (0, 0))],
        out_specs=tok,
        out_shape=jax.ShapeDtypeStruct((n, D_MODEL), F32),
        scratch_shapes=[pltpu.VMEM((seq, D_MODEL), BF16)],
        compiler_params=_params(1),
        name="context_attention",
    )(x, mods, q, k, v, wo_bf16)


NA_ROWS = 8
NA_TOKENS = NA_ROWS * GRID_W
BAND_TOKENS = WIN_ROWS * GRID_W


def _na_kernel(x_ref, mod_ref, q_ref, kp_ref, kc_ref, kn_ref, vp_ref, vc_ref, vn_ref,
               kctx_ref, vctx_ref, bias_ref, wo_ref, o_ref, kband_ref, vband_ref, att_ref,
               *, n_grid_rows):
    rb = pl.program_id(1)
    for j, (kr, vr) in enumerate(((kp_ref, vp_ref), (kc_ref, vc_ref), (kn_ref, vn_ref))):
        kband_ref[j * NA_TOKENS:(j + 1) * NA_TOKENS, :] = kr[...]
        vband_ref[j * NA_TOKENS:(j + 1) * NA_TOKENS, :] = vr[...]

    def one_row(i, carry):
        r = rb * NA_ROWS + i
        rs = jnp.clip(r - WIN_ROWS // 2, 0, n_grid_rows - WIN_ROWS)
        band0 = pl.multiple_of((rs - (rb - 1) * NA_ROWS) * GRID_W, GRID_W)
        q0 = pl.multiple_of(i * GRID_W, GRID_W)
        bias0 = rs - r + (WIN_ROWS - 1)
        for hp in range(N_HEADS // 2):
            cols = slice(hp * LANES, (hp + 1) * LANES)
            k2 = kband_ref[pl.ds(band0, BAND_TOKENS), cols]
            v2 = vband_ref[pl.ds(band0, BAND_TOKENS), cols]
            kc2 = kctx_ref[:, cols]
            vc2 = vctx_ref[:, cols]

            def scores(q, head):
                s = _dot_nt(q, k2)
                blocks = [s[:, c * LANES:(c + 1) * LANES] + bias_ref[head, bias0 + 2 * c]
                          for c in range(BAND_TOKENS // LANES)]
                return blocks + [_dot_nt(q, kc2)]

            def values(p):
                pl_loc = jnp.concatenate(p[:-1], axis=-1)
                return (jnp.dot(pl_loc, v2, preferred_element_type=F32)
                        + jnp.dot(p[-1], vc2, preferred_element_type=F32))

            q2 = q_ref[pl.ds(q0, GRID_W), cols]
            lane = lax.broadcasted_iota(jnp.int32, q2.shape, 1)
            first = lane < HEAD_DIM
            outs = []
            for sub, keep in enumerate((first, jnp.logical_not(first))):
                s = scores(jnp.where(keep, q2, jnp.zeros_like(q2)), 2 * hp + sub)
                m = functools.reduce(jnp.maximum,
                                     [jnp.max(b, axis=-1, keepdims=True) for b in s])
                p = [jnp.exp(b - m) for b in s]
                l = functools.reduce(jnp.add, [jnp.sum(b, axis=-1, keepdims=True) for b in p])
                outs.append(values([b.astype(BF16) for b in p]) / l)
            att_ref[pl.ds(q0, GRID_W), cols] = jnp.where(first, outs[0], outs[1]).astype(BF16)
        return carry

    lax.fori_loop(0, NA_ROWS, one_row, 0)
    y = jnp.dot(att_ref[...], wo_ref[...], preferred_element_type=F32)
    o_ref[...] = x_ref[...] + mod_ref[...][2:3] * y


def _neighbourhood_attention(x, mods, layer, q, k, v, kctx, vctx, bias, wo_bf16, batch, seq):
    n_grid_rows = seq // GRID_W
    nrb = n_grid_rows // NA_ROWS
    cur = lambda b, r: (b * nrb + r, 0)
    prev = lambda b, r: (b * nrb + jnp.maximum(r - 1, 0), 0)
    nxt = lambda b, r: (b * nrb + jnp.minimum(r + 1, nrb - 1), 0)
    blk = lambda f: pl.BlockSpec((NA_TOKENS, D_MODEL), f)
    ctx = pl.BlockSpec((None, kctx.shape[1], D_MODEL), lambda b, r: (b, 0, 0))
    return pl.pallas_call(
        functools.partial(_na_kernel, n_grid_rows=n_grid_rows),
        grid=(batch, nrb),
        in_specs=[blk(cur),
                  pl.BlockSpec((None, None, 6, D_MODEL), lambda b, r: (layer, 1 + b, 0, 0)),
                  blk(cur), blk(prev), blk(cur), blk(nxt), blk(prev), blk(cur), blk(nxt),
                  ctx, ctx,
                  pl.BlockSpec(bias.shape, lambda b, r: (0, 0, 0, 0)),
                  pl.BlockSpec((D_MODEL, D_MODEL), lambda b, r: (0, 0))],
        out_specs=blk(cur),
        out_shape=jax.ShapeDtypeStruct(x.shape, F32),
        scratch_shapes=[pltpu.VMEM((3 * NA_TOKENS, D_MODEL), BF16),
                        pltpu.VMEM((3 * NA_TOKENS, D_MODEL), BF16),
                        pltpu.VMEM((NA_TOKENS, D_MODEL), BF16)],
        compiler_params=_params(2),
        name="neighbourhood_attention",
    )(x, mods, q, k, k, k, v, v, v, kctx, vctx, bias, wo_bf16)


def _na_bias_table(rpb_layer):
    qc = jnp.arange(GRID_W)[:, None]
    kc = jnp.arange(GRID_W)[None, :]
    start = jnp.clip(qc - WIN_COLS // 2, 0, GRID_W - WIN_COLS)
    valid = (kc >= start) & (kc < start + WIN_COLS)
    off = jnp.clip(kc - qc + WIN_COLS - 1, 0, 2 * WIN_COLS - 2)
    m = jnp.where(valid[None, None], rpb_layer[:, :, off], NEG)
    return jnp.concatenate([m[:, :-1], m[:, 1:]], axis=-1)


def _pool_kernel(x_ref, xp_ref, xn_ref, mod_ref, g_ref, w_ref, ps_ref, o_ref, h_ref,
                 *, tile, seq):
    i = pl.program_id(0)
    tiles_per_seq = seq // tile
    t_in_seq = i % tiles_per_seq
    m = mod_ref[...]
    g = g_ref[...]
    h_cur = _rms_modulate(x_ref[...], g, m[0:1], m[1:2])
    h_prev = _rms_modulate(xp_ref[...], g, m[0:1], m[1:2])
    h_next = _rms_modulate(xn_ref[...], g, m[0:1], m[1:2])
    h_ref[0:POOL_HALO, :] = jnp.where(t_in_seq > 0, h_prev, 0.0)
    h_ref[POOL_HALO:POOL_HALO + tile, :] = h_cur
    h_ref[POOL_HALO + tile:, :] = jnp.where(t_in_seq < tiles_per_seq - 1, h_next, 0.0)

    pos = t_in_seq * tile + lax.broadcasted_iota(jnp.int32, (tile, 1), 0)
    ys = []
    for grp, w in enumerate(POOL_SIZES):
        cols = slice(grp * POOL_GROUP_DIM, (grp + 1) * POOL_GROUP_DIM)
        total = h_ref[POOL_HALO - w // 2:POOL_HALO - w // 2 + tile, cols]
        for d in range(1 - w // 2, w - w // 2):
            total = total + h_ref[POOL_HALO + d:POOL_HALO + d + tile, cols]
        lo = jnp.clip(pos - w // 2, 0, seq)
        hi = jnp.clip(pos - w // 2 + w, 0, seq)
        pooled = total / (hi - lo).astype(F32)
        diff = (pooled - h_cur[:, cols]).astype(BF16)
        ys.append(jnp.dot(diff, w_ref[grp], preferred_element_type=F32))
    y = jnp.concatenate(ys, axis=-1) * ps_ref[...]
    o_ref[...] = x_ref[...] + m[2:3] * y


def _pool_mixer(x, mods, layer, tiles_per_cond_row, g, w_pool_bf16, pool_scale, tile, seq):
    n = x.shape[0]
    hb = tile // POOL_HALO
    last = n // POOL_HALO - 1
    const = lambda i: (0, 0)
    return pl.pallas_call(
        functools.partial(_pool_kernel, tile=tile, seq=seq),
        grid=(n // tile,),
        in_specs=[pl.BlockSpec((tile, D_MODEL), lambda i: (i, 0)),
                  pl.BlockSpec((POOL_HALO, D_MODEL), lambda i: (jnp.maximum(i * hb - 1, 0), 0)),
                  pl.BlockSpec((POOL_HALO, D_MODEL),
                               lambda i: (jnp.minimum((i + 1) * hb, last), 0)),
                  _mod_spec(layer, tiles_per_cond_row),
                  pl.BlockSpec((1, D_MODEL), const),
                  pl.BlockSpec(w_pool_bf16.shape, lambda i: (0, 0, 0)),
                  pl.BlockSpec((1, D_MODEL), const)],
        out_specs=pl.BlockSpec((tile, D_MODEL), lambda i: (i, 0)),
        out_shape=jax.ShapeDtypeStruct(x.shape, F32),
        scratch_shapes=[pltpu.VMEM((tile + 2 * POOL_HALO, D_MODEL), F32)],
        compiler_params=_params(1),
        name="pool_mixer",
    )(x, x, x, mods, g, w_pool_bf16, pool_scale)


def _first_argmax(vals):
    best = functools.reduce(jnp.maximum, vals)
    idx = jnp.full(best.shape, len(vals) - 1, jnp.int32)
    for j in range(len(vals) - 2, -1, -1):
        idx = jnp.where(vals[j] == best, j, idx)
    return best, idx


def _router_kernel(x_ref, mod_ref, g_ref, wr_hi_ref, wr_lo_ref, br_ref, h_ref, aux_ref):
    m = mod_ref[...]
    h = _rms_modulate(x_ref[...], g_ref[...], m[3:4], m[4:5])
    h_ref[...] = h
    h_hi = h.astype(BF16)
    h_lo = (h - h_hi.astype(F32)).astype(BF16)
    logits = (_dot_nt(wr_hi_ref[...], h_hi) + _dot_nt(wr_hi_ref[...], h_lo)
              + _dot_nt(wr_lo_ref[...], h_hi)) + br_ref[...]
    row = lambda r: logits[r:r + 1, :]
    grp = [row(j) for j in range(N_EXPERT_GROUPS)]
    g_max, g_sel = _first_argmax(grp)
    g_w = 1.0 / functools.reduce(jnp.add, [jnp.exp(v - g_max) for v in grp])
    e_in = []
    for j in range(EXPERTS_PER_GROUP):
        v = row(N_EXPERT_GROUPS + (N_EXPERT_GROUPS - 1) * EXPERTS_PER_GROUP + j)
        for gi in range(N_EXPERT_GROUPS - 2, -1, -1):
            v = jnp.where(g_sel == gi, row(N_EXPERT_GROUPS + gi * EXPERTS_PER_GROUP + j), v)
        e_in.append(v)
    v1, i1 = _first_argmax(e_in)
    v2, i2 = _first_argmax([jnp.where(i1 == j, -jnp.inf, e_in[j])
                            for j in range(EXPERTS_PER_GROUP)])
    t = jnp.exp(v2 - v1)
    w1 = g_w / (1.0 + t)
    w2 = g_w * t / (1.0 + t)
    lo = jnp.minimum(i1, i2)
    hi = jnp.maximum(i1, i2)
    w_lo = jnp.where(i1 < i2, w1, w2)
    w_hi = jnp.where(i1 < i2, w2, w1)
    pair = jnp.where(lo == 0, hi - 1, jnp.where(lo == 1, hi + 1, 5))
    bucket = (g_sel * N_PAIRS + pair).astype(F32)
    pad = jnp.zeros((5, bucket.shape[1]), F32)
    aux_ref[...] = jnp.concatenate([bucket, w_lo, w_hi, pad], axis=0)


def _router(x, mods, layer, tiles_per_cond_row, g, wr_hi, wr_lo, br):
    n = x.shape[0]
    const = lambda i: (0, 0)
    tok = pl.BlockSpec((TOKEN_TILE, D_MODEL), lambda i: (i, 0))
    return pl.pallas_call(
        _router_kernel,
        grid=(n // TOKEN_TILE,),
        in_specs=[tok, _mod_spec(layer, tiles_per_cond_row),
                  pl.BlockSpec((1, D_MODEL), const),
                  pl.BlockSpec((ROUTER_ROWS, D_MODEL), const),
                  pl.BlockSpec((ROUTER_ROWS, D_MODEL), const),
                  pl.BlockSpec((ROUTER_ROWS, 1), const)],
        out_specs=[tok, pl.BlockSpec((8, TOKEN_TILE), lambda i: (0, i))],
        out_shape=[jax.ShapeDtypeStruct((n, D_MODEL), F32),
                   jax.ShapeDtypeStruct((8, n), F32)],
        compiler_params=_params(1),
        name="moe_router",
    )(x, mods, g, wr_hi, wr_lo, br)


def _gather_rows(idx_ref, base, src_hbm, buf, sem, n_rows):
    def issue(r, carry):
        pltpu.make_async_copy(src_hbm.at[pl.ds(idx_ref[base + r], 1)],
                              buf.at[pl.ds(r, 1)], sem).start()
        return carry
    lax.fori_loop(0, n_rows, issue, 0, unroll=8)
    pltpu.make_async_copy(src_hbm.at[pl.ds(0, n_rows)], buf, sem).wait()


def _expert_kernel(perm_ref, elo_ref, ehi_ref, nused_ref, h_hbm, cw_ref,
                   wg_lo, wu_lo, wd_lo, wg_hi, wu_hi, wd_hi, o_ref, xbuf, sem):
    t = pl.program_id(0)

    @pl.when(t < nused_ref[0])
    def _():
        _gather_rows(perm_ref, t * MOE_TILE, h_hbm, xbuf, sem, MOE_TILE)
        x = xbuf[...].astype(BF16)
        cw = cw_ref[...]
        y = None
        for e, (wg, wu, wd) in enumerate(((wg_lo, wu_lo, wd_lo), (wg_hi, wu_hi, wd_hi))):
            a = jnp.dot(x, wg[...], preferred_element_type=F32)
            u = jnp.dot(x, wu[...], preferred_element_type=F32)
            act = (a / (1.0 + jnp.exp(-a))) * u * cw[:, e:e + 1]
            part = jnp.dot(act.astype(BF16), wd[...], preferred_element_type=F32)
            y = part if y is None else y + part
        o_ref[...] = y

    @pl.when(t >= nused_ref[0])
    def _():
        o_ref[...] = jnp.zeros_like(o_ref)


def _experts(h, perm, e_lo, e_hi, n_used, cw, wg, wu, wd):
    n_pad = perm.shape[0]
    row = lambda t, *_: (t, 0)
    w_lo = lambda t, perm, elo, ehi, nu: (elo[t], 0, 0)
    w_hi = lambda t, perm, elo, ehi, nu: (ehi[t], 0, 0)
    up = lambda f: pl.BlockSpec((None, D_MODEL, D_FF), f)
    down = lambda f: pl.BlockSpec((None, D_FF, D_MODEL), f)
    return pl.pallas_call(
        _expert_kernel,
        grid_spec=pltpu.PrefetchScalarGridSpec(
            num_scalar_prefetch=4,
            grid=(n_pad // MOE_TILE,),
            in_specs=[pl.BlockSpec(memory_space=pl.ANY),
                      pl.BlockSpec((MOE_TILE, 2), row),
                      up(w_lo), up(w_lo), down(w_lo), up(w_hi), up(w_hi), down(w_hi)],
            out_specs=pl.BlockSpec((MOE_TILE, D_MODEL), row),
            scratch_shapes=[pltpu.VMEM((MOE_TILE, D_MODEL), F32),
                            pltpu.SemaphoreType.DMA(())]),
        out_shape=jax.ShapeDtypeStruct((n_pad, D_MODEL), F32),
        compiler_params=_params(1),
        name="moe_experts",
    )(perm, e_lo, e_hi, n_used, h, cw, wg, wu, wd, wg, wu, wd)


def _unsort_kernel(dest_ref, x_ref, mod_ref, y_hbm, o_ref, ybuf, sem):
    i = pl.program_id(0)
    _gather_rows(dest_ref, i * TOKEN_TILE, y_hbm, ybuf, sem, TOKEN_TILE)
    o_ref[...] = x_ref[...] + mod_ref[...][5:6] * ybuf[...]


def _unsort_residual(x, mods, layer, tiles_per_cond_row, dest, y_sorted):
    n = x.shape[0]
    tok = pl.BlockSpec((TOKEN_TILE, D_MODEL), lambda i, *_: (i, 0))
    return pl.pallas_call(
        _unsort_kernel,
        grid_spec=pltpu.PrefetchScalarGridSpec(
            num_scalar_prefetch=1,
            grid=(n // TOKEN_TILE,),
            in_specs=[tok, _mod_spec(layer, tiles_per_cond_row),
                      pl.BlockSpec(memory_space=pl.ANY)],
            out_specs=tok,
            scratch_shapes=[pltpu.VMEM((TOKEN_TILE, D_MODEL), F32),
                            pltpu.SemaphoreType.DMA(())]),
        out_shape=jax.ShapeDtypeStruct(x.shape, F32),
        compiler_params=_params(1),
        name="moe_unsort_residual",
    )(dest, x, mods, y_sorted)


_PAIR_LO = (0, 0, 0, 1, 1, 2)
_PAIR_HI = (1, 2, 3, 2, 3, 3)


def _dispatch_plan(aux, n):
    n_tiles = n // MOE_TILE + N_BUCKETS
    n_pad = n_tiles * MOE_TILE
    bucket = aux[0].astype(jnp.int32)
    onehot = (bucket[:, None] == jnp.arange(N_BUCKETS)[None, :]).astype(jnp.int32)
    rank = jnp.take_along_axis(jnp.cumsum(onehot, axis=0), bucket[:, None], axis=1)[:, 0] - 1
    counts = jnp.sum(onehot, axis=0)
    tiles = (counts + MOE_TILE - 1) // MOE_TILE
    tile_end = jnp.cumsum(tiles)
    tile_start = tile_end - tiles
    n_used = tile_end[-1]
    dest = (tile_start[bucket] * MOE_TILE + rank).astype(jnp.int32)
    perm = jnp.zeros((n_pad,), jnp.int32).at[dest].set(jnp.arange(n, dtype=jnp.int32))
    cw = jnp.zeros((n_pad, 2), F32).at[dest].set(aux[1:3].T)
    t = jnp.minimum(jnp.arange(n_tiles), n_used - 1)
    tile_bucket = jnp.sum((t[:, None] >= tile_end[None, :]).astype(jnp.int32), axis=1)
    grp = tile_bucket // N_PAIRS
    pair = tile_bucket % N_PAIRS
    e_lo = grp * EXPERTS_PER_GROUP + jnp.asarray(_PAIR_LO, jnp.int32)[pair]
    e_hi = grp * EXPERTS_PER_GROUP + jnp.asarray(_PAIR_HI, jnp.int32)[pair]
    return (perm, dest, e_lo.astype(jnp.int32), e_hi.astype(jnp.int32),
            n_used.astype(jnp.int32).reshape(1), cw)


def _moe(x, mods, layer, tiles_per_cond_row, g, router_w, expert_w):
    h, aux = _router(x, mods, layer, tiles_per_cond_row, g, *router_w)
    perm, dest, e_lo, e_hi, n_used, cw = _dispatch_plan(aux, x.shape[0])
    y_sorted = _experts(h, perm, e_lo, e_hi, n_used, cw, *expert_w)
    return _unsort_residual(x, mods, layer, tiles_per_cond_row, dest, y_sorted)


def kernel(x_prompt, x_sample, cache_k, cache_v, c, c_ctx, norm1_g, norm2_g, w_mod, b_mod,
           w_qkv, q_norm_g, k_norm_g, rpb, w_o, w_pool, pool_scale, w_router_group,
           b_router_group, w_router_expert, b_router_expert, w_gate, w_up, w_down):
    bp, lp, _ = x_prompt.shape
    bs, ls, _ = x_sample.shape
    depth = w_mod.shape[0]
    assert bs + 1 <= COND_ROWS and ls % TOKEN_TILE == 0 and (bp * lp) % TOKEN_TILE == 0
    assert lp <= TOKEN_TILE and TOKEN_TILE % lp == 0

    cond = jnp.zeros((COND_ROWS, D_MODEL), F32).at[0].set(c_ctx).at[1:1 + bs].set(c)
    mods = _modulation(cond, w_mod, b_mod)

    xp = x_prompt.reshape(bp * lp, D_MODEL)
    xs = x_sample.reshape(bs * ls, D_MODEL)
    s_tiles = ls // TOKEN_TILE

    head_of = jnp.arange(MXU_DIM) // HEAD_DIM
    ones_bd = (head_of[:, None] == head_of[None, :]).astype(BF16)

    new_k, new_v = [], []
    for l in range(depth):
        j = l // 2
        g1 = norm1_g[l][None]
        if l % 2 == 0:
            w_bf16 = w_qkv[j].astype(BF16)
            wo_bf16 = w_o[j].astype(BF16)
            qg = jnp.tile(q_norm_g[j], N_HEADS)[None]
            kg = jnp.tile(k_norm_g[j], N_HEADS)[None]
            qp, kp, vp = _qkv(xp, mods, l, None, g1, w_bf16, qg, kg, ones_bd, F32)
            new_k.append(kp.reshape(bp, lp, N_HEADS, HEAD_DIM))
            new_v.append(vp.reshape(bp, lp, N_HEADS, HEAD_DIM))
            xp = _ctx_attention(xp, mods, l, qp, kp, vp, wo_bf16, lp)
            qs, ks, vs = _qkv(xs, mods, l, s_tiles, g1, w_bf16, qg, kg, ones_bd, BF16)
            kctx = cache_k[:, j].reshape(bs, -1, D_MODEL).astype(BF16)
            vctx = cache_v[:, j].reshape(bs, -1, D_MODEL).astype(BF16)
            xs = _neighbourhood_attention(xs, mods, l, qs, ks, vs, kctx, vctx,
                                          _na_bias_table(rpb[j]), wo_bf16, bs, ls)
        else:
            wp = w_pool[j].astype(BF16)
            ps = pool_scale[j][None]
            xp = _pool_mixer(xp, mods, l, None, g1, wp, ps, lp, lp)
            xs = _pool_mixer(xs, mods, l, s_tiles, g1, wp, ps, TOKEN_TILE, ls)

        wr = jnp.zeros((ROUTER_ROWS, D_MODEL), F32)
        wr = wr.at[:N_EXPERT_GROUPS].set(w_router_group[l].T)
        wr = wr.at[N_EXPERT_GROUPS:N_EXPERT_GROUPS + N_EXPERTS].set(w_router_expert[l].T)
        wr_hi = wr.astype(BF16)
        wr_lo = (wr - wr_hi.astype(F32)).astype(BF16)
        br = jnp.zeros((ROUTER_ROWS, 1), F32)
        br = br.at[:N_EXPERT_GROUPS, 0].set(b_router_group[l])
        br = br.at[N_EXPERT_GROUPS:N_EXPERT_GROUPS + N_EXPERTS, 0].set(b_router_expert[l])
        router_w = (wr_hi, wr_lo, br)
        expert_w = (w_gate[l].astype(BF16), w_up[l].astype(BF16), w_down[l].astype(BF16))
        g2 = norm2_g[l][None]
        xp = _moe(xp, mods, l, None, g2, router_w, expert_w)
        xs = _moe(xs, mods, l, s_tiles, g2, router_w, expert_w)

    return (xp.reshape(bp, lp, D_MODEL), xs.reshape(bs, ls, D_MODEL),
            jnp.stack(new_k, axis=1), jnp.stack(new_v, axis=1))
```

```python
import functools

import jax
import jax.numpy as jnp
from jax import lax
from jax.experimental import pallas as pl
from jax.experimental.pallas import tpu as pltpu

D_MODEL = 1024
N_HEADS = 16
HEAD_DIM = D_MODEL // N_HEADS
GRID_W = 64
WIN_ROWS = 8
WIN_COLS = 16
POOL_SIZES = (2, 4, 8, 16)
POOL_GROUP_DIM = D_MODEL // len(POOL_SIZES)
POOL_HALO = 8
N_EXPERT_GROUPS = 4
EXPERTS_PER_GROUP = 4
N_EXPERTS = N_EXPERT_GROUPS * EXPERTS_PER_GROUP
N_PAIRS = 6
N_BUCKETS = N_EXPERT_GROUPS * N_PAIRS
D_FF = D_MODEL // 2
EPS = 1e-6
NEG = -1e30

LANES = 128
MXU_DIM = 256
TOKEN_TILE = 512
MOE_TILE = 256
ROUTER_ROWS = 32
COND_ROWS = 16
VMEM_LIMIT = 56 * 1024 * 1024

BF16 = jnp.bfloat16
F32 = jnp.float32


def _params(n_grid_dims, vmem=VMEM_LIMIT):
    return pltpu.CompilerParams(
        dimension_semantics=("arbitrary",) * n_grid_dims, vmem_limit_bytes=vmem)


def _rms_modulate(x, g, shift, scale):
    y = x * lax.rsqrt(jnp.mean(x * x, axis=-1, keepdims=True) + EPS)
    return (y * g) * (1.0 + scale) + shift


def _mod_kernel(cond_ref, w_ref, b_ref, o_ref):
    c = cond_ref[...]
    s = c / (1.0 + jnp.exp(-c))
    o_ref[...] = jnp.dot(s, w_ref[...], preferred_element_type=F32,
                         precision=lax.Precision.HIGHEST) + b_ref[...]


def _modulation(cond, w_mod, b_mod):
    depth = w_mod.shape[0]
    tn = 1536
    out = pl.pallas_call(
        _mod_kernel,
        grid=(depth, 6 * D_MODEL // tn),
        in_specs=[
            pl.BlockSpec((COND_ROWS, D_MODEL), lambda l, n: (0, 0)),
            pl.BlockSpec((None, D_MODEL, tn), lambda l, n: (l, 0, n)),
            pl.BlockSpec((None, 1, tn), lambda l, n: (l, 0, n)),
        ],
        out_specs=pl.BlockSpec((None, COND_ROWS, tn), lambda l, n: (l, 0, n)),
        out_shape=jax.ShapeDtypeStruct((depth, COND_ROWS, 6 * D_MODEL), F32),
        compiler_params=_params(2),
        name="adaln_modulation",
    )(cond, w_mod, b_mod.reshape(depth, 1, 6 * D_MODEL))
    return out.reshape(depth, COND_ROWS, 6, D_MODEL)


def _mod_spec(layer, tiles_per_cond_row):
    if tiles_per_cond_row is None:
        return pl.BlockSpec((None, None, 6, D_MODEL), lambda i, *_: (layer, 0, 0, 0))
    return pl.BlockSpec((None, None, 6, D_MODEL),
                        lambda i, *_: (layer, 1 + i // tiles_per_cond_row, 0, 0))


def _head_rms(t, ones_bd, gain):
    sq = (t * t).astype(BF16)
    parts = [jnp.dot(sq[:, c:c + MXU_DIM], ones_bd, preferred_element_type=F32)
             for c in range(0, D_MODEL, MXU_DIM)]
    mean = jnp.concatenate(parts, axis=-1) / HEAD_DIM
    return t * lax.rsqrt(mean + EPS) * gain


def _qkv_kernel(x_ref, mod_ref, g_ref, wqk_ref, wv_ref, qg_ref, kg_ref, bd_ref,
                q_ref, k_ref, v_ref, *, v_transposed):
    m = mod_ref[...]
    h = _rms_modulate(x_ref[...], g_ref[...], m[0:1], m[1:2]).astype(BF16)
    qk = jnp.dot(h, wqk_ref[...], preferred_element_type=F32)
    bd = bd_ref[...]
    q = _head_rms(qk[:, :D_MODEL], bd, qg_ref[...])
    k = _head_rms(qk[:, D_MODEL:], bd, kg_ref[...])
    q_ref[...] = (q * HEAD_DIM ** -0.5).astype(q_ref.dtype)
    k_ref[...] = k.astype(k_ref.dtype)
    if v_transposed:
        v_ref[...] = _dot_nt(wv_ref[...], h).astype(v_ref.dtype)
    else:
        v_ref[...] = jnp.dot(h, wv_ref[...], preferred_element_type=F32).astype(v_ref.dtype)


def _qkv(x, mods, layer, tiles_per_cond_row, g, wqk_bf16, wv_bf16, qg, kg, bd, kv_dtype,
         v_transposed):
    n = x.shape[0]
    row = lambda i: (i, 0)
    const = lambda i: (0, 0)
    tok = pl.BlockSpec((TOKEN_TILE, D_MODEL), row)
    vec = pl.BlockSpec((1, D_MODEL), const)
    if v_transposed:
        v_spec = pl.BlockSpec((D_MODEL, TOKEN_TILE), lambda i: (0, i))
        v_shape = jax.ShapeDtypeStruct((D_MODEL, n), kv_dtype)
    else:
        v_spec, v_shape = tok, jax.ShapeDtypeStruct((n, D_MODEL), kv_dtype)
    return pl.pallas_call(
        functools.partial(_qkv_kernel, v_transposed=v_transposed),
        grid=(n // TOKEN_TILE,),
        in_specs=[tok, _mod_spec(layer, tiles_per_cond_row), vec,
                  pl.BlockSpec((D_MODEL, 2 * D_MODEL), const),
                  pl.BlockSpec((D_MODEL, D_MODEL), const), vec, vec,
                  pl.BlockSpec((MXU_DIM, MXU_DIM), const)],
        out_specs=[tok, tok, v_spec],
        out_shape=[jax.ShapeDtypeStruct((n, D_MODEL), BF16),
                   jax.ShapeDtypeStruct((n, D_MODEL), kv_dtype), v_shape],
        compiler_params=_params(1),
        name="qkv_proj",
    )(x, mods, g, wqk_bf16, wv_bf16, qg, kg, bd)


def _dot_nt(a, b):
    return lax.dot_general(a, b, (((1,), (1,)), ((), ())), preferred_element_type=F32)


def _pair_attention(q2, score_fn, value_fn):
    lane = lax.broadcasted_iota(jnp.int32, q2.shape, 1)
    first = lane < HEAD_DIM
    outs = []
    for keep in (first, jnp.logical_not(first)):
        s = score_fn(jnp.where(keep, q2, jnp.zeros_like(q2)))
        m = functools.reduce(jnp.maximum, [jnp.max(b, axis=-1, keepdims=True) for b in s])
        p = [jnp.exp(b - m) for b in s]
        l = functools.reduce(jnp.add, [jnp.sum(b, axis=-1, keepdims=True) for b in p])
        outs.append(value_fn([b.astype(BF16) for b in p]) / l)
    return jnp.where(first, outs[0], outs[1])


def _ctx_attn_kernel(x_ref, mod_ref, q_ref, k_ref, v_ref, wo_ref, o_ref, att_ref):
    for hp in range(N_HEADS // 2):
        cols = slice(hp * LANES, (hp + 1) * LANES)
        k2 = k_ref[:, cols].astype(BF16)
        v2 = v_ref[:, cols].astype(BF16)
        att = _pair_attention(
            q_ref[:, cols],
            lambda q: [_dot_nt(q, k2)],
            lambda p: jnp.dot(p[0], v2, preferred_element_type=F32))
        att_ref[:, cols] = att.astype(BF16)
    y = jnp.dot(att_ref[...], wo_ref[...], preferred_element_type=F32)
    o_ref[...] = x_ref[...] + mod_ref[...][2:3] * y


def _ctx_attention(x, mods, layer, q, k, v, wo_bf16, seq):
    n = x.shape[0]
    row = lambda i: (i, 0)
    tok = pl.BlockSpec((seq, D_MODEL), row)
    return pl.pallas_call(
        _ctx_attn_kernel,
        grid=(n // seq,),
        in_specs=[tok, _mod_spec(layer, None), tok, tok, tok,
                  pl.BlockSpec((D_MODEL, D_MODEL), lambda i: (0, 0))],
        out_specs=tok,
        out_shape=jax.ShapeDtypeStruct((n, D_MODEL), F32),
        scratch_shapes=[pltpu.VMEM((seq, D_MODEL), BF16)],
        compiler_params=_params(1),
        name="context_attention",
    )(x, mods, q, k, v, wo_bf16)


NA_ROWS = 4
NA_TOKENS = NA_ROWS * GRID_W
NA_SLOTS = 3 * NA_ROWS
BIAS_NONE = 2 * WIN_ROWS - 1


def _na_kernel(x_ref, mod_ref, q_ref, kp_ref, kc_ref, kn_ref, vp_ref, vc_ref, vn_ref,
               kctx_ref, vctx_ref, bias_ref, wo_ref, o_ref, att_ref, *, n_grid_rows):
    rb = pl.program_id(1)
    entry = []
    for s in range(NA_SLOTS):
        kr = (rb - 1) * NA_ROWS + s
        per_row = []
        for i in range(NA_ROWS):
            r = rb * NA_ROWS + i
            rs = jnp.clip(r - WIN_ROWS // 2, 0, n_grid_rows - WIN_ROWS)
            in_band = jnp.logical_and(kr >= rs, kr < rs + WIN_ROWS)
            per_row.append(jnp.where(in_band, s - i + (WIN_ROWS - 1 - NA_ROWS), BIAS_NONE))
        entry.append(per_row)

    lane = lax.broadcasted_iota(jnp.int32, (GRID_W, LANES), 1)
    left = lane < GRID_W
    qlane = lax.broadcasted_iota(jnp.int32, (NA_TOKENS, LANES), 1)
    k_blocks = (kp_ref, kc_ref, kn_ref)
    v_blocks = (vp_ref, vc_ref, vn_ref)
    for hp in range(N_HEADS // 2):
        cols = slice(hp * LANES, (hp + 1) * LANES)
        q2 = q_ref[:, cols]
        halves = []
        for sub in range(2):
            head = 2 * hp + sub
            keep = (qlane < HEAD_DIM) if sub == 0 else (qlane >= HEAD_DIM)
            qm = jnp.where(keep, q2, jnp.zeros_like(q2))
            s_blocks = []
            for j in range(3):
                st = _dot_nt(k_blocks[j][:, cols], qm)
                rows = []
                for sr in range(NA_ROWS):
                    s = j * NA_ROWS + sr
                    bias = jnp.concatenate(
                        [jnp.where(left, bias_ref[head, entry[s][2 * t]],
                                   bias_ref[head, entry[s][2 * t + 1]])
                         for t in range(NA_ROWS // 2)], axis=-1)
                    rows.append(st[sr * GRID_W:(sr + 1) * GRID_W, :] + bias)
                s_blocks.append(jnp.concatenate(rows, axis=0))
            s_blocks.append(_dot_nt(kctx_ref[:, cols], qm))
            m = functools.reduce(jnp.maximum,
                                 [jnp.max(b, axis=0, keepdims=True) for b in s_blocks])
            p = [jnp.exp(b - m) for b in s_blocks]
            l = functools.reduce(jnp.add, [jnp.sum(b, axis=0, keepdims=True) for b in p])
            vrows = slice(hp * LANES + sub * HEAD_DIM, hp * LANES + (sub + 1) * HEAD_DIM)
            vt = [vb[vrows, :] for vb in v_blocks] + [vctx_ref[vrows, :]]
            ot = functools.reduce(jnp.add, [
                jnp.dot(vt[j], p[j].astype(BF16), preferred_element_type=F32)
                for j in range(4)])
            halves.append(ot / l)
        att_ref[:, cols] = jnp.concatenate(halves, axis=0).T.astype(BF16)
    y = jnp.dot(att_ref[...], wo_ref[...], preferred_element_type=F32)
    o_ref[...] = x_ref[...] + mod_ref[...][2:3] * y


def _neighbourhood_attention(x, mods, layer, q, k, vt, kctx, vctx_t, bias, wo_bf16, batch, seq):
    n_grid_rows = seq // GRID_W
    nrb = n_grid_rows // NA_ROWS
    cur = lambda b, r: (b * nrb + r, 0)
    prev = lambda b, r: (b * nrb + jnp.maximum(r - 1, 0), 0)
    nxt = lambda b, r: (b * nrb + jnp.minimum(r + 1, nrb - 1), 0)
    blk = lambda f: pl.BlockSpec((NA_TOKENS, D_MODEL), f)
    swap = lambda f: (lambda b, r: f(b, r)[::-1])
    blk_t = lambda f: pl.BlockSpec((D_MODEL, NA_TOKENS), swap(f))
    n_ctx = kctx.shape[1]
    return pl.pallas_call(
        functools.partial(_na_kernel, n_grid_rows=n_grid_rows),
        grid=(batch, nrb),
        in_specs=[blk(cur),
                  pl.BlockSpec((None, None, 6, D_MODEL), lambda b, r: (layer, 1 + b, 0, 0)),
                  blk(cur), blk(prev), blk(cur), blk(nxt),
                  blk_t(prev), blk_t(cur), blk_t(nxt),
                  pl.BlockSpec((None, n_ctx, D_MODEL), lambda b, r: (b, 0, 0)),
                  pl.BlockSpec((None, D_MODEL, n_ctx), lambda b, r: (b, 0, 0)),
                  pl.BlockSpec(bias.shape, lambda b, r: (0, 0, 0, 0)),
                  pl.BlockSpec((D_MODEL, D_MODEL), lambda b, r: (0, 0))],
        out_specs=blk(cur),
        out_shape=jax.ShapeDtypeStruct(x.shape, F32),
        scratch_shapes=[pltpu.VMEM((NA_TOKENS, D_MODEL), BF16)],
        compiler_params=_params(2),
        name="neighbourhood_attention",
    )(x, mods, q, k, k, k, vt, vt, vt, kctx, vctx_t, bias, wo_bf16)


def _na_bias_table(rpb_layer):
    qc = jnp.arange(GRID_W)[None, :]
    kc = jnp.arange(GRID_W)[:, None]
    start = jnp.clip(qc - WIN_COLS // 2, 0, GRID_W - WIN_COLS)
    valid = (kc >= start) & (kc < start + WIN_COLS)
    off = jnp.clip(kc - qc + WIN_COLS - 1, 0, 2 * WIN_COLS - 2)
    m = jnp.where(valid[None, None], rpb_layer[:, :, off], NEG)
    m = jnp.concatenate([m, jnp.full_like(m[:, :1], NEG)], axis=1)
    return jnp.concatenate([m, m], axis=-1)


def _pool_kernel(x_ref, xp_ref, xn_ref, mod_ref, g_ref, w_ref, ps_ref, o_ref, h_ref,
                 *, tile, seq):
    i = pl.program_id(0)
    tiles_per_seq = seq // tile
    t_in_seq = i % tiles_per_seq
    m = mod_ref[...]
    g = g_ref[...]
    h_cur = _rms_modulate(x_ref[...], g, m[0:1], m[1:2])
    h_prev = _rms_modulate(xp_ref[...], g, m[0:1], m[1:2])
    h_next = _rms_modulate(xn_ref[...], g, m[0:1], m[1:2])
    h_ref[0:POOL_HALO, :] = jnp.where(t_in_seq > 0, h_prev, 0.0)
    h_ref[POOL_HALO:POOL_HALO + tile, :] = h_cur
    h_ref[POOL_HALO + tile:, :] = jnp.where(t_in_seq < tiles_per_seq - 1, h_next, 0.0)

    pos = t_in_seq * tile + lax.broadcasted_iota(jnp.int32, (tile, 1), 0)
    ys = []
    for grp, w in enumerate(POOL_SIZES):
        cols = slice(grp * POOL_GROUP_DIM, (grp + 1) * POOL_GROUP_DIM)
        total = h_ref[POOL_HALO - w // 2:POOL_HALO - w // 2 + tile, cols]
        for d in range(1 - w // 2, w - w // 2):
            total = total + h_ref[POOL_HALO + d:POOL_HALO + d + tile, cols]
        lo = jnp.clip(pos - w // 2, 0, seq)
        hi = jnp.clip(pos - w // 2 + w, 0, seq)
        pooled = total / (hi - lo).astype(F32)
        diff = (pooled - h_cur[:, cols]).astype(BF16)
        ys.append(jnp.dot(diff, w_ref[grp], preferred_element_type=F32))
    y = jnp.concatenate(ys, axis=-1) * ps_ref[...]
    o_ref[...] = x_ref[...] + m[2:3] * y


def _pool_mixer(x, mods, layer, tiles_per_cond_row, g, w_pool_bf16, pool_scale, tile, seq):
    n = x.shape[0]
    hb = tile // POOL_HALO
    last = n // POOL_HALO - 1
    const = lambda i: (0, 0)
    return pl.pallas_call(
        functools.partial(_pool_kernel, tile=tile, seq=seq),
        grid=(n // tile,),
        in_specs=[pl.BlockSpec((tile, D_MODEL), lambda i: (i, 0)),
                  pl.BlockSpec((POOL_HALO, D_MODEL), lambda i: (jnp.maximum(i * hb - 1, 0), 0)),
                  pl.BlockSpec((POOL_HALO, D_MODEL),
                               lambda i: (jnp.minimum((i + 1) * hb, last), 0)),
                  _mod_spec(layer, tiles_per_cond_row),
                  pl.BlockSpec((1, D_MODEL), const),
                  pl.BlockSpec(w_pool_bf16.shape, lambda i: (0, 0, 0)),
                  pl.BlockSpec((1, D_MODEL), const)],
        out_specs=pl.BlockSpec((tile, D_MODEL), lambda i: (i, 0)),
        out_shape=jax.ShapeDtypeStruct(x.shape, F32),
        scratch_shapes=[pltpu.VMEM((tile + 2 * POOL_HALO, D_MODEL), F32)],
        compiler_params=_params(1),
        name="pool_mixer",
    )(x, x, x, mods, g, w_pool_bf16, pool_scale)


def _first_argmax(vals):
    best = functools.reduce(jnp.maximum, vals)
    idx = jnp.full(best.shape, len(vals) - 1, jnp.int32)
    for j in range(len(vals) - 2, -1, -1):
        idx = jnp.where(vals[j] == best, j, idx)
    return best, idx


def _router_kernel(x_ref, mod_ref, g_ref, wr_hi_ref, wr_lo_ref, br_ref, aux_ref):
    m = mod_ref[...]
    h = _rms_modulate(x_ref[...], g_ref[...], m[3:4], m[4:5])
    h_hi = h.astype(BF16)
    h_lo = (h - h_hi.astype(F32)).astype(BF16)
    logits = (_dot_nt(wr_hi_ref[...], h_hi) + _dot_nt(wr_hi_ref[...], h_lo)
              + _dot_nt(wr_lo_ref[...], h_hi)) + br_ref[...]
    row = lambda r: logits[r:r + 1, :]
    grp = [row(j) for j in range(N_EXPERT_GROUPS)]
    g_max, g_sel = _first_argmax(grp)
    g_w = 1.0 / functools.reduce(jnp.add, [jnp.exp(v - g_max) for v in grp])
    e_in = []
    for j in range(EXPERTS_PER_GROUP):
        v = row(N_EXPERT_GROUPS + (N_EXPERT_GROUPS - 1) * EXPERTS_PER_GROUP + j)
        for gi in range(N_EXPERT_GROUPS - 2, -1, -1):
            v = jnp.where(g_sel == gi, row(N_EXPERT_GROUPS + gi * EXPERTS_PER_GROUP + j), v)
        e_in.append(v)
    v1, i1 = _first_argmax(e_in)
    v2, i2 = _first_argmax([jnp.where(i1 == j, -jnp.inf, e_in[j])
                            for j in range(EXPERTS_PER_GROUP)])
    t = jnp.exp(v2 - v1)
    w1 = g_w / (1.0 + t)
    w2 = g_w * t / (1.0 + t)
    lo = jnp.minimum(i1, i2)
    hi = jnp.maximum(i1, i2)
    w_lo = jnp.where(i1 < i2, w1, w2)
    w_hi = jnp.where(i1 < i2, w2, w1)
    pair = jnp.where(lo == 0, hi - 1, jnp.where(lo == 1, hi + 1, 5))
    bucket = (g_sel * N_PAIRS + pair).astype(F32)
    pad = jnp.zeros((5, bucket.shape[1]), F32)
    aux_ref[...] = jnp.concatenate([bucket, w_lo, w_hi, pad], axis=0)


def _router(x, mods, layer, tiles_per_cond_row, g, wr_hi, wr_lo, br):
    n = x.shape[0]
    const = lambda i: (0, 0)
    tok = pl.BlockSpec((TOKEN_TILE, D_MODEL), lambda i: (i, 0))
    return pl.pallas_call(
        _router_kernel,
        grid=(n // TOKEN_TILE,),
        in_specs=[tok, _mod_spec(layer, tiles_per_cond_row),
                  pl.BlockSpec((1, D_MODEL), const),
                  pl.BlockSpec((ROUTER_ROWS, D_MODEL), const),
                  pl.BlockSpec((ROUTER_ROWS, D_MODEL), const),
                  pl.BlockSpec((ROUTER_ROWS, 1), const)],
        out_specs=pl.BlockSpec((8, TOKEN_TILE), lambda i: (0, i)),
        out_shape=jax.ShapeDtypeStruct((8, n), F32),
        compiler_params=_params(1),
        name="moe_router",
    )(x, mods, g, wr_hi, wr_lo, br)


ROW_W = D_MODEL + LANES


def _row_copies(idx_ref, base, hbm, buf, sem, n_rows, to_hbm):
    def issue(r, carry):
        far = hbm.at[pl.ds(idx_ref[base + r], 1)]
        near = buf.at[pl.ds(r, 1)]
        src, dst = (near, far) if to_hbm else (far, near)
        pltpu.make_async_copy(src, dst, sem).start()
        return carry
    lax.fori_loop(0, n_rows, issue, 0, unroll=8)


def _wait_rows(hbm, buf, sem):
    pltpu.make_async_copy(hbm.at[pl.ds(0, buf.shape[0])], buf, sem).wait()


def _dispatch_kernel(dest_ref, zero_at_ref, x_ref, mod_ref, g_ref, cw_ref, hs_hbm,
                     rowbuf, zbuf, sems, zsem):
    i = pl.program_id(0)
    last = pl.num_programs(0) - 1
    slot = i % 2

    @pl.when(i == 0)
    def _():
        zbuf[...] = jnp.zeros_like(zbuf)

        def clear(row0):
            return pltpu.make_async_copy(
                zbuf, hs_hbm.at[pl.ds(pl.multiple_of(row0, MOE_TILE), MOE_TILE)], zsem)

        n_tiles = hs_hbm.shape[0] // MOE_TILE
        n_used = zero_at_ref[N_BUCKETS]
        for b in range(N_BUCKETS):
            @pl.when(zero_at_ref[b] >= 0)
            def _():
                clear(zero_at_ref[b]).start()
        lax.fori_loop(n_used, n_tiles, lambda t, c: (clear(t * MOE_TILE).start(), c)[1], 0)
        for b in range(N_BUCKETS):
            @pl.when(zero_at_ref[b] >= 0)
            def _():
                clear(zero_at_ref[b]).wait()
        lax.fori_loop(n_used, n_tiles, lambda t, c: (clear(t * MOE_TILE).wait(), c)[1], 0)

    @pl.when(i >= 2)
    def _():
        _wait_rows(hs_hbm, rowbuf.at[slot], sems.at[slot])

    m = mod_ref[...]
    rowbuf[slot, :, :D_MODEL] = _rms_modulate(x_ref[...], g_ref[...], m[3:4], m[4:5])
    rowbuf[slot, :, D_MODEL:] = cw_ref[...]
    _row_copies(dest_ref, i * TOKEN_TILE, hs_hbm, rowbuf.at[slot], sems.at[slot],
                TOKEN_TILE, to_hbm=True)

    @pl.when(i == last)
    def _():
        @pl.when(i >= 1)
        def _():
            _wait_rows(hs_hbm, rowbuf.at[1 - slot], sems.at[1 - slot])
        _wait_rows(hs_hbm, rowbuf.at[slot], sems.at[slot])


def _dispatch(x, mods, layer, tiles_per_cond_row, g, dest, zero_at, cw_nat, n_pad):
    n = x.shape[0]
    tok = lambda w: pl.BlockSpec((TOKEN_TILE, w), lambda i, *_: (i, 0))
    return pl.pallas_call(
        _dispatch_kernel,
        grid_spec=pltpu.PrefetchScalarGridSpec(
            num_scalar_prefetch=2,
            grid=(n // TOKEN_TILE,),
            in_specs=[tok(D_MODEL), _mod_spec(layer, tiles_per_cond_row),
                      pl.BlockSpec((1, D_MODEL), lambda i, *_: (0, 0)), tok(LANES)],
            out_specs=pl.BlockSpec(memory_space=pl.ANY),
            scratch_shapes=[pltpu.VMEM((2, TOKEN_TILE, ROW_W), F32),
                            pltpu.VMEM((MOE_TILE, ROW_W), F32),
                            pltpu.SemaphoreType.DMA((2,)),
                            pltpu.SemaphoreType.DMA(())]),
        out_shape=jax.ShapeDtypeStruct((n_pad, ROW_W), F32),
        compiler_params=_params(1),
        name="moe_dispatch",
    )(dest, zero_at, x, mods, g, cw_nat)


def _expert_kernel(elo_ref, ehi_ref, nused_ref, hs_ref,
                   wg_lo, wu_lo, wd_lo, wg_hi, wu_hi, wd_hi, o_ref):
    t = pl.program_id(0)

    @pl.when(t < nused_ref[0])
    def _():
        x = hs_ref[:, :D_MODEL].astype(BF16)
        cw = hs_ref[:, D_MODEL:]
        y = None
        for e, (wg, wu, wd) in enumerate(((wg_lo, wu_lo, wd_lo), (wg_hi, wu_hi, wd_hi))):
            a = jnp.dot(x, wg[...], preferred_element_type=F32)
            u = jnp.dot(x, wu[...], preferred_element_type=F32)
            act = (a / (1.0 + jnp.exp(-a))) * u * cw[:, e:e + 1]
            part = jnp.dot(act.astype(BF16), wd[...], preferred_element_type=F32)
            y = part if y is None else y + part
        o_ref[...] = y

    @pl.when(t >= nused_ref[0])
    def _():
        o_ref[...] = jnp.zeros_like(o_ref)


def _experts(hs, e_lo, e_hi, n_used, wg, wu, wd):
    n_pad = hs.shape[0]
    row_in = lambda t, elo, ehi, nu: (jnp.minimum(t, nu[0] - 1), 0)
    row_out = lambda t, *_: (t, 0)
    w_lo = lambda t, elo, ehi, nu: (elo[t], 0, 0)
    w_hi = lambda t, elo, ehi, nu: (ehi[t], 0, 0)
    up = lambda f: pl.BlockSpec((None, D_MODEL, D_FF), f)
    down = lambda f: pl.BlockSpec((None, D_FF, D_MODEL), f)
    return pl.pallas_call(
        _expert_kernel,
        grid_spec=pltpu.PrefetchScalarGridSpec(
            num_scalar_prefetch=3,
            grid=(n_pad // MOE_TILE,),
            in_specs=[pl.BlockSpec((MOE_TILE, ROW_W), row_in),
                      up(w_lo), up(w_lo), down(w_lo), up(w_hi), up(w_hi), down(w_hi)],
            out_specs=pl.BlockSpec((MOE_TILE, D_MODEL), row_out)),
        out_shape=jax.ShapeDtypeStruct((n_pad, D_MODEL), F32),
        compiler_params=_params(1),
        name="moe_experts",
    )(e_lo, e_hi, n_used, hs, wg, wu, wd, wg, wu, wd)


def _unsort_kernel(dest_ref, x_ref, mod_ref, y_hbm, o_ref, ybuf, sems):
    i = pl.program_id(0)
    slot = i % 2

    def fetch(tile, s):
        _row_copies(dest_ref, tile * TOKEN_TILE, y_hbm, ybuf.at[s], sems.at[s],
                    TOKEN_TILE, to_hbm=False)

    @pl.when(i == 0)
    def _():
        fetch(0, 0)

    @pl.when(i + 1 < pl.num_programs(0))
    def _():
        fetch(i + 1, 1 - slot)

    _wait_rows(y_hbm, ybuf.at[slot], sems.at[slot])
    o_ref[...] = x_ref[...] + mod_ref[...][5:6] * ybuf[slot]


def _unsort_residual(x, mods, layer, tiles_per_cond_row, dest, y_sorted):
    n = x.shape[0]
    tok = pl.BlockSpec((TOKEN_TILE, D_MODEL), lambda i, *_: (i, 0))
    return pl.pallas_call(
        _unsort_kernel,
        grid_spec=pltpu.PrefetchScalarGridSpec(
            num_scalar_prefetch=1,
            grid=(n // TOKEN_TILE,),
            in_specs=[tok, _mod_spec(layer, tiles_per_cond_row),
                      pl.BlockSpec(memory_space=pl.ANY)],
            out_specs=tok,
            scratch_shapes=[pltpu.VMEM((2, TOKEN_TILE, D_MODEL), F32),
                            pltpu.SemaphoreType.DMA((2,))]),
        out_shape=jax.ShapeDtypeStruct(x.shape, F32),
        compiler_params=_params(1),
        name="moe_unsort_residual",
    )(dest, x, mods, y_sorted)


_PAIR_LO = (0, 0, 0, 1, 1, 2)
_PAIR_HI = (1, 2, 3, 2, 3, 3)


def _dispatch_plan(aux, n):
    n_tiles = n // MOE_TILE + N_BUCKETS
    bucket = aux[0].astype(jnp.int32)
    onehot = (bucket[:, None] == jnp.arange(N_BUCKETS)[None, :]).astype(jnp.int32)
    csum = jnp.cumsum(onehot, axis=0)
    counts = csum[-1]
    tiles = (counts + MOE_TILE - 1) // MOE_TILE
    tile_end = jnp.cumsum(tiles)
    tile_start = tile_end - tiles
    n_used = tile_end[-1]
    slot0 = tile_start * MOE_TILE - 1
    dest = jnp.sum(onehot * (csum + slot0[None, :]), axis=1).astype(jnp.int32)
    zero_at = jnp.where(tiles > 0, (tile_end - 1) * MOE_TILE, -1)
    zero_at = jnp.concatenate([zero_at, n_used[None]]).astype(jnp.int32)
    t = jnp.minimum(jnp.arange(n_tiles), n_used - 1)
    tile_bucket = jnp.sum((t[:, None] >= tile_end[None, :]).astype(jnp.int32), axis=1)
    grp = tile_bucket // N_PAIRS
    pair = tile_bucket % N_PAIRS
    e_lo = grp * EXPERTS_PER_GROUP + jnp.asarray(_PAIR_LO, jnp.int32)[pair]
    e_hi = grp * EXPERTS_PER_GROUP + jnp.asarray(_PAIR_HI, jnp.int32)[pair]
    cw_nat = jnp.pad(aux[1:3].T, ((0, 0), (0, LANES - 2)))
    return (dest, zero_at, e_lo.astype(jnp.int32), e_hi.astype(jnp.int32),
            n_used.astype(jnp.int32).reshape(1), cw_nat, n_tiles * MOE_TILE)


def _moe(x, mods, layer, tiles_per_cond_row, g, router_w, expert_w):
    aux = _router(x, mods, layer, tiles_per_cond_row, g, *router_w)
    dest, zero_at, e_lo, e_hi, n_used, cw_nat, n_pad = _dispatch_plan(aux, x.shape[0])
    hs = _dispatch(x, mods, layer, tiles_per_cond_row, g, dest, zero_at, cw_nat, n_pad)
    y_sorted = _experts(hs, e_lo, e_hi, n_used, *expert_w)
    return _unsort_residual(x, mods, layer, tiles_per_cond_row, dest, y_sorted)


def kernel(x_prompt, x_sample, cache_k, cache_v, c, c_ctx, norm1_g, norm2_g, w_mod, b_mod,
           w_qkv, q_norm_g, k_norm_g, rpb, w_o, w_pool, pool_scale, w_router_group,
           b_router_group, w_router_expert, b_router_expert, w_gate, w_up, w_down):
    bp, lp, _ = x_prompt.shape
    bs, ls, _ = x_sample.shape
    depth = w_mod.shape[0]
    assert bs + 1 <= COND_ROWS and ls % TOKEN_TILE == 0 and (bp * lp) % TOKEN_TILE == 0
    assert lp <= TOKEN_TILE and TOKEN_TILE % lp == 0

    cond = jnp.zeros((COND_ROWS, D_MODEL), F32).at[0].set(c_ctx).at[1:1 + bs].set(c)
    mods = _modulation(cond, w_mod, b_mod)

    xp = x_prompt.reshape(bp * lp, D_MODEL)
    xs = x_sample.reshape(bs * ls, D_MODEL)
    s_tiles = ls // TOKEN_TILE

    head_of = jnp.arange(MXU_DIM) // HEAD_DIM
    ones_bd = (head_of[:, None] == head_of[None, :]).astype(BF16)

    new_k, new_v = [], []
    for l in range(depth):
        j = l // 2
        g1 = norm1_g[l][None]
        if l % 2 == 0:
            wqk = w_qkv[j][:, :2 * D_MODEL].astype(BF16)
            wv = w_qkv[j][:, 2 * D_MODEL:].astype(BF16)
            wo_bf16 = w_o[j].astype(BF16)
            qg = jnp.tile(q_norm_g[j], N_HEADS)[None]
            kg = jnp.tile(k_norm_g[j], N_HEADS)[None]
            qp, kp, vp = _qkv(xp, mods, l, None, g1, wqk, wv, qg, kg, ones_bd, F32, False)
            new_k.append(kp.reshape(bp, lp, N_HEADS, HEAD_DIM))
            new_v.append(vp.reshape(bp, lp, N_HEADS, HEAD_DIM))
            xp = _ctx_attention(xp, mods, l, qp, kp, vp, wo_bf16, lp)
            qs, ks, vts = _qkv(xs, mods, l, s_tiles, g1, wqk, wv.T, qg, kg, ones_bd, BF16, True)
            kctx = cache_k[:, j].reshape(bs, -1, D_MODEL).astype(BF16)
            vctx_t = jnp.swapaxes(cache_v[:, j].reshape(bs, -1, D_MODEL), 1, 2).astype(BF16)
            xs = _neighbourhood_attention(xs, mods, l, qs, ks, vts, kctx, vctx_t,
                                          _na_bias_table(rpb[j]), wo_bf16, bs, ls)
        else:
            wp = w_pool[j].astype(BF16)
            ps = pool_scale[j][None]
            xp = _pool_mixer(xp, mods, l, None, g1, wp, ps, lp, lp)
            xs = _pool_mixer(xs, mods, l, s_tiles, g1, wp, ps, TOKEN_TILE, ls)

        wr = jnp.zeros((ROUTER_ROWS, D_MODEL), F32)
        wr = wr.at[:N_EXPERT_GROUPS].set(w_router_group[l].T)
        wr = wr.at[N_EXPERT_GROUPS:N_EXPERT_GROUPS + N_EXPERTS].set(w_router_expert[l].T)
        wr_hi = wr.astype(BF16)
        wr_lo = (wr - wr_hi.astype(F32)).astype(BF16)
        br = jnp.zeros((ROUTER_ROWS, 1), F32)
        br = br.at[:N_EXPERT_GROUPS, 0].set(b_router_group[l])
        br = br.at[N_EXPERT_GROUPS:N_EXPERT_GROUPS + N_EXPERTS, 0].set(b_router_expert[l])
        router_w = (wr_hi, wr_lo, br)
        expert_w = (w_gate[l].astype(BF16), w_up[l].astype(BF16), w_down[l].astype(BF16))
        g2 = norm2_g[l][None]
        xp = _moe(xp, mods, l, None, g2, router_w, expert_w)
        xs = _moe(xs, mods, l, s_tiles, g2, router_w, expert_w)

    return (xp.reshape(bp, lp, D_MODEL), xs.reshape(bs, ls, D_MODEL),
            jnp.stack(new_k, axis=1), jnp.stack(new_v, axis=1))
```

```python
import functools

import jax
import jax.numpy as jnp
import numpy as np
from jax import lax
from jax.experimental import pallas as pl
from jax.experimental.pallas import tpu as pltpu

D_MODEL = 1024
N_HEADS = 16
HEAD_DIM = D_MODEL // N_HEADS
GRID_W = 64
WIN_ROWS = 8
WIN_COLS = 16
POOL_SIZES = (2, 4, 8, 16)
POOL_GROUP_DIM = D_MODEL // len(POOL_SIZES)
POOL_HALO = 8
N_EXPERT_GROUPS = 4
EXPERTS_PER_GROUP = 4
N_EXPERTS = N_EXPERT_GROUPS * EXPERTS_PER_GROUP
N_PAIRS = 6
N_BUCKETS = N_EXPERT_GROUPS * N_PAIRS
D_FF = D_MODEL // 2
EPS = 1e-6
NEG = -1e30

LANES = 128
MXU_DIM = 256
TOKEN_TILE = 512
MOE_TILE = 256
ROUTER_ROWS = 32
COND_ROWS = 16
VMEM_LIMIT = 56 * 1024 * 1024

BF16 = jnp.bfloat16
F32 = jnp.float32


def _params(n_grid_dims, vmem=VMEM_LIMIT):
    return pltpu.CompilerParams(
        dimension_semantics=("arbitrary",) * n_grid_dims, vmem_limit_bytes=vmem)


def _rms_modulate(x, g, shift, scale):
    y = x * lax.rsqrt(jnp.mean(x * x, axis=-1, keepdims=True) + EPS)
    return (y * g) * (1.0 + scale) + shift


def _mod_kernel(cond_ref, w_ref, b_ref, o_ref):
    c = cond_ref[...]
    s = c / (1.0 + jnp.exp(-c))
    o_ref[...] = jnp.dot(s, w_ref[...], preferred_element_type=F32,
                         precision=lax.Precision.HIGHEST) + b_ref[...]


def _modulation(cond, w_mod, b_mod):
    depth = w_mod.shape[0]
    tn = 1536
    out = pl.pallas_call(
        _mod_kernel,
        grid=(depth, 6 * D_MODEL // tn),
        in_specs=[
            pl.BlockSpec((COND_ROWS, D_MODEL), lambda l, n: (0, 0)),
            pl.BlockSpec((None, D_MODEL, tn), lambda l, n: (l, 0, n)),
            pl.BlockSpec((None, 1, tn), lambda l, n: (l, 0, n)),
        ],
        out_specs=pl.BlockSpec((None, COND_ROWS, tn), lambda l, n: (l, 0, n)),
        out_shape=jax.ShapeDtypeStruct((depth, COND_ROWS, 6 * D_MODEL), F32),
        compiler_params=_params(2),
        name="adaln_modulation",
    )(cond, w_mod, b_mod.reshape(depth, 1, 6 * D_MODEL))
    return out.reshape(depth, COND_ROWS, 6, D_MODEL)


def _mod_spec(layer, tiles_per_cond_row):
    if tiles_per_cond_row is None:
        return pl.BlockSpec((None, None, 6, D_MODEL), lambda i, *_: (layer, 0, 0, 0))
    return pl.BlockSpec((None, None, 6, D_MODEL),
                        lambda i, *_: (layer, 1 + i // tiles_per_cond_row, 0, 0))


def _head_rms(t, ones_bd, gain):
    sq = (t * t).astype(BF16)
    parts = [jnp.dot(sq[:, c:c + MXU_DIM], ones_bd, preferred_element_type=F32)
             for c in range(0, D_MODEL, MXU_DIM)]
    mean = jnp.concatenate(parts, axis=-1) / HEAD_DIM
    return t * lax.rsqrt(mean + EPS) * gain


def _qkv_kernel(x_ref, mod_ref, g_ref, wqk_ref, wv_ref, qg_ref, kg_ref, bd_ref,
                q_ref, k_ref, v_ref, *, q_scale, v_transposed):
    m = mod_ref[...]
    h = _rms_modulate(x_ref[...], g_ref[...], m[0:1], m[1:2]).astype(BF16)
    qk = jnp.dot(h, wqk_ref[...], preferred_element_type=F32)
    bd = bd_ref[...]
    q = _head_rms(qk[:, :D_MODEL], bd, qg_ref[...])
    k = _head_rms(qk[:, D_MODEL:], bd, kg_ref[...])
    q_ref[...] = (q * q_scale).astype(q_ref.dtype)
    k_ref[...] = k.astype(k_ref.dtype)
    if v_transposed:
        vt = _dot_nt(wv_ref[...], h).astype(v_ref.dtype)
        for j in range(TOKEN_TILE // NA_TOKENS):
            v_ref[j] = vt[:, j * NA_TOKENS:(j + 1) * NA_TOKENS]
    else:
        v_ref[...] = jnp.dot(h, wv_ref[...], preferred_element_type=F32).astype(v_ref.dtype)


def _qkv(x, mods, layer, tiles_per_cond_row, g, w_bf16, wvt_bf16, qg, kg, bd, kv_dtype,
         q_scale):
    n = x.shape[0]
    row = lambda i: (i, 0)
    const = lambda i: (0, 0)
    tok = pl.BlockSpec((TOKEN_TILE, D_MODEL), row)
    vec = pl.BlockSpec((1, D_MODEL), const)
    if wvt_bf16 is not None:
        per_tile = TOKEN_TILE // NA_TOKENS
        v_spec = pl.BlockSpec((per_tile, D_MODEL, NA_TOKENS), lambda i: (i, 0, 0))
        v_shape = jax.ShapeDtypeStruct((n // NA_TOKENS, D_MODEL, NA_TOKENS), kv_dtype)
        wv, wv_spec = wvt_bf16, pl.BlockSpec((D_MODEL, D_MODEL), const)
    else:
        v_spec, v_shape = tok, jax.ShapeDtypeStruct((n, D_MODEL), kv_dtype)
        wv, wv_spec = w_bf16, pl.BlockSpec((D_MODEL, D_MODEL), lambda i: (0, 2))
    return pl.pallas_call(
        functools.partial(_qkv_kernel, q_scale=q_scale, v_transposed=wvt_bf16 is not None),
        grid=(n // TOKEN_TILE,),
        in_specs=[tok, _mod_spec(layer, tiles_per_cond_row), vec,
                  pl.BlockSpec((D_MODEL, 2 * D_MODEL), const), wv_spec, vec, vec,
                  pl.BlockSpec((MXU_DIM, MXU_DIM), const)],
        out_specs=[tok, tok, v_spec],
        out_shape=[jax.ShapeDtypeStruct((n, D_MODEL), BF16),
                   jax.ShapeDtypeStruct((n, D_MODEL), kv_dtype), v_shape],
        compiler_params=_params(1),
        name="qkv_proj",
    )(x, mods, g, w_bf16, wv, qg, kg, bd)


def _dot_nt(a, b):
    return lax.dot_general(a, b, (((1,), (1,)), ((), ())), preferred_element_type=F32)


def _pair_attention(q2, score_fn, value_fn):
    lane = lax.broadcasted_iota(jnp.int32, q2.shape, 1)
    first = lane < HEAD_DIM
    outs = []
    for keep in (first, jnp.logical_not(first)):
        s = score_fn(jnp.where(keep, q2, jnp.zeros_like(q2)))
        m = functools.reduce(jnp.maximum, [jnp.max(b, axis=-1, keepdims=True) for b in s])
        p = [jnp.exp(b - m) for b in s]
        l = functools.reduce(jnp.add, [jnp.sum(b, axis=-1, keepdims=True) for b in p])
        outs.append(value_fn([b.astype(BF16) for b in p]) / l)
    return jnp.where(first, outs[0], outs[1])


def _ctx_attn_kernel(x_ref, mod_ref, q_ref, k_ref, v_ref, wo_ref, o_ref, att_ref):
    for hp in range(N_HEADS // 2):
        cols = slice(hp * LANES, (hp + 1) * LANES)
        k2 = k_ref[:, cols].astype(BF16)
        v2 = v_ref[:, cols].astype(BF16)
        att = _pair_attention(
            q_ref[:, cols],
            lambda q: [_dot_nt(q, k2)],
            lambda p: jnp.dot(p[0], v2, preferred_element_type=F32))
        att_ref[:, cols] = att.astype(BF16)
    y = jnp.dot(att_ref[...], wo_ref[...], preferred_element_type=F32)
    o_ref[...] = x_ref[...] + mod_ref[...][2:3] * y


def _ctx_attention(x, mods, layer, q, k, v, wo_bf16, seq):
    n = x.shape[0]
    row = lambda i: (i, 0)
    tok = pl.BlockSpec((seq, D_MODEL), row)
    return pl.pallas_call(
        _ctx_attn_kernel,
        grid=(n // seq,),
        in_specs=[tok, _mod_spec(layer, None), tok, tok, tok,
                  pl.BlockSpec((D_MODEL, D_MODEL), lambda i: (0, 0))],
        out_specs=tok,
        out_shape=jax.ShapeDtypeStruct((n, D_MODEL), F32),
        scratch_shapes=[pltpu.VMEM((seq, D_MODEL), BF16)],
        compiler_params=_params(1),
        name="context_attention",
    )(x, mods, q, k, v, wo_bf16)


NA_ROWS = 4
NA_TOKENS = NA_ROWS * GRID_W
NA_SLOTS = 3 * NA_ROWS
BIAS_NONE = 2 * WIN_ROWS - 1
ONES_ROWS = 16
LOG2_E = 1.4426950408889634


def _na_kernel(x_ref, mod_ref, q_ref, kp_ref, kc_ref, kn_ref, vp_ref, vc_ref, vn_ref,
               kctx_ref, vctx_ref, bias_ref, wo_ref, o_ref, att_ref, *, n_grid_rows):
    rb = pl.program_id(1)
    entry = []
    for s in range(NA_SLOTS):
        kr = (rb - 1) * NA_ROWS + s
        per_row = []
        for i in range(NA_ROWS):
            r = rb * NA_ROWS + i
            rs = jnp.clip(r - WIN_ROWS // 2, 0, n_grid_rows - WIN_ROWS)
            in_band = jnp.logical_and(kr >= rs, kr < rs + WIN_ROWS)
            per_row.append(jnp.where(in_band, s - i + (WIN_ROWS - 1 - NA_ROWS), BIAS_NONE))
        entry.append(per_row)

    lane = lax.broadcasted_iota(jnp.int32, (GRID_W, LANES), 1)
    left = lane < GRID_W
    qlane = lax.broadcasted_iota(jnp.int32, (NA_TOKENS, LANES), 1)
    k_blocks = (kp_ref, kc_ref, kn_ref)
    v_blocks = (vp_ref, vc_ref, vn_ref)
    for hp in range(N_HEADS // 2):
        cols = slice(hp * LANES, (hp + 1) * LANES)
        q2 = q_ref[:, cols]
        halves = []
        for sub in range(2):
            head = 2 * hp + sub
            keep = (qlane < HEAD_DIM) if sub == 0 else (qlane >= HEAD_DIM)
            qm = jnp.where(keep, q2, jnp.zeros_like(q2))
            s_blocks = []
            for j in range(3):
                st = _dot_nt(k_blocks[j][:, cols], qm)
                rows = []
                for sr in range(NA_ROWS):
                    s = j * NA_ROWS + sr
                    bias = jnp.concatenate(
                        [jnp.where(left, bias_ref[head, entry[s][2 * t]],
                                   bias_ref[head, entry[s][2 * t + 1]])
                         for t in range(NA_ROWS // 2)], axis=-1)
                    rows.append(st[sr * GRID_W:(sr + 1) * GRID_W, :] + bias)
                s_blocks.append(jnp.concatenate(rows, axis=0))
            s_blocks.append(_dot_nt(kctx_ref[:, cols], qm))
            m = functools.reduce(jnp.maximum,
                                 [jnp.max(b, axis=0, keepdims=True) for b in s_blocks])
            p = [jnp.exp2(b - m).astype(BF16) for b in s_blocks]
            vrows = slice(hp * LANES + sub * HEAD_DIM, hp * LANES + (sub + 1) * HEAD_DIM)
            vt = [jnp.concatenate([vb[vrows, :], jnp.ones((ONES_ROWS, vb.shape[1]), BF16)],
                                  axis=0) for vb in v_blocks + (vctx_ref,)]
            ot = functools.reduce(jnp.add, [
                jnp.dot(vt[j], p[j], preferred_element_type=F32) for j in range(4)])
            halves.append(ot[:HEAD_DIM] / ot[HEAD_DIM:HEAD_DIM + 1])
        att_ref[:, cols] = jnp.concatenate(halves, axis=0).T.astype(BF16)
    y = jnp.dot(att_ref[...], wo_ref[...], preferred_element_type=F32)
    o_ref[...] = x_ref[...] + mod_ref[...][2:3] * y


def _neighbourhood_attention(x, mods, layer, q, k, vt, kctx, vctx_t, bias, wo_bf16, batch, seq):
    n_grid_rows = seq // GRID_W
    nrb = n_grid_rows // NA_ROWS
    cur = lambda b, r: (b * nrb + r, 0)
    prev = lambda b, r: (b * nrb + jnp.maximum(r - 1, 0), 0)
    nxt = lambda b, r: (b * nrb + jnp.minimum(r + 1, nrb - 1), 0)
    blk = lambda f: pl.BlockSpec((NA_TOKENS, D_MODEL), f)
    blk_t = lambda f: pl.BlockSpec((None, D_MODEL, NA_TOKENS),
                                   lambda b, r: (f(b, r)[0], 0, 0))
    n_ctx = kctx.shape[1]
    return pl.pallas_call(
        functools.partial(_na_kernel, n_grid_rows=n_grid_rows),
        grid=(batch, nrb),
        in_specs=[blk(cur),
                  pl.BlockSpec((None, None, 6, D_MODEL), lambda b, r: (layer, 1 + b, 0, 0)),
                  blk(cur), blk(prev), blk(cur), blk(nxt),
                  blk_t(prev), blk_t(cur), blk_t(nxt),
                  pl.BlockSpec((None, n_ctx, D_MODEL), lambda b, r: (b, 0, 0)),
                  pl.BlockSpec((None, D_MODEL, n_ctx), lambda b, r: (b, 0, 0)),
                  pl.BlockSpec(bias.shape, lambda b, r: (0, 0, 0, 0)),
                  pl.BlockSpec((D_MODEL, D_MODEL), lambda b, r: (0, 0))],
        out_specs=blk(cur),
        out_shape=jax.ShapeDtypeStruct(x.shape, F32),
        scratch_shapes=[pltpu.VMEM((NA_TOKENS, D_MODEL), BF16)],
        compiler_params=_params(2),
        name="neighbourhood_attention",
    )(x, mods, q, k, k, k, vt, vt, vt, kctx, vctx_t, bias, wo_bf16)


def _na_bias_table(rpb_layer):
    qc = np.arange(GRID_W)[None, :]
    kc = np.arange(GRID_W)[:, None]
    start = np.clip(qc - WIN_COLS // 2, 0, GRID_W - WIN_COLS)
    valid = (kc >= start) & (kc < start + WIN_COLS)
    off = kc - qc + WIN_COLS - 1
    select = ((off[None] == np.arange(2 * WIN_COLS - 1)[:, None, None]) & valid[None])
    m = jnp.einsum('hro,okq->hrkq', rpb_layer * LOG2_E, jnp.asarray(select, F32),
                   precision=lax.Precision.HIGHEST)
    m = m + jnp.asarray(np.where(valid, 0.0, NEG), F32)
    m = jnp.concatenate([m, jnp.full_like(m[:, :1], NEG)], axis=1)
    return jnp.concatenate([m, m], axis=-1)


def _pool_kernel(x_ref, xp_ref, xn_ref, mod_ref, g_ref, w_ref, ps_ref, o_ref, h_ref,
                 *, tile, seq):
    i = pl.program_id(0)
    tiles_per_seq = seq // tile
    t_in_seq = i % tiles_per_seq
    m = mod_ref[...]
    g = g_ref[...]
    h_cur = _rms_modulate(x_ref[...], g, m[0:1], m[1:2])
    h_prev = _rms_modulate(xp_ref[...], g, m[0:1], m[1:2])
    h_next = _rms_modulate(xn_ref[...], g, m[0:1], m[1:2])
    h_ref[0:POOL_HALO, :] = jnp.where(t_in_seq > 0, h_prev, 0.0)
    h_ref[POOL_HALO:POOL_HALO + tile, :] = h_cur
    h_ref[POOL_HALO + tile:, :] = jnp.where(t_in_seq < tiles_per_seq - 1, h_next, 0.0)

    pos = t_in_seq * tile + lax.broadcasted_iota(jnp.int32, (tile, 1), 0)
    ys = []
    for grp, w in enumerate(POOL_SIZES):
        cols = slice(grp * POOL_GROUP_DIM, (grp + 1) * POOL_GROUP_DIM)
        total = h_ref[POOL_HALO - w // 2:POOL_HALO - w // 2 + tile, cols]
        for d in range(1 - w // 2, w - w // 2):
            total = total + h_ref[POOL_HALO + d:POOL_HALO + d + tile, cols]
        lo = jnp.clip(pos - w // 2, 0, seq)
        hi = jnp.clip(pos - w // 2 + w, 0, seq)
        pooled = total / (hi - lo).astype(F32)
        diff = (pooled - h_cur[:, cols]).astype(BF16)
        ys.append(jnp.dot(diff, w_ref[grp], preferred_element_type=F32))
    y = jnp.concatenate(ys, axis=-1) * ps_ref[...]
    o_ref[...] = x_ref[...] + m[2:3] * y


def _pool_mixer(x, mods, layer, tiles_per_cond_row, g, w_pool_bf16, pool_scale, tile, seq):
    n = x.shape[0]
    hb = tile // POOL_HALO
    last = n // POOL_HALO - 1
    const = lambda i: (0, 0)
    return pl.pallas_call(
        functools.partial(_pool_kernel, tile=tile, seq=seq),
        grid=(n // tile,),
        in_specs=[pl.BlockSpec((tile, D_MODEL), lambda i: (i, 0)),
                  pl.BlockSpec((POOL_HALO, D_MODEL), lambda i: (jnp.maximum(i * hb - 1, 0), 0)),
                  pl.BlockSpec((POOL_HALO, D_MODEL),
                               lambda i: (jnp.minimum((i + 1) * hb, last), 0)),
                  _mod_spec(layer, tiles_per_cond_row),
                  pl.BlockSpec((1, D_MODEL), const),
                  pl.BlockSpec(w_pool_bf16.shape, lambda i: (0, 0, 0)),
                  pl.BlockSpec((1, D_MODEL), const)],
        out_specs=pl.BlockSpec((tile, D_MODEL), lambda i: (i, 0)),
        out_shape=jax.ShapeDtypeStruct(x.shape, F32),
        scratch_shapes=[pltpu.VMEM((tile + 2 * POOL_HALO, D_MODEL), F32)],
        compiler_params=_params(1),
        name="pool_mixer",
    )(x, x, x, mods, g, w_pool_bf16, pool_scale)


def _first_argmax(vals):
    best = functools.reduce(jnp.maximum, vals)
    idx = jnp.full(best.shape, len(vals) - 1, jnp.int32)
    for j in range(len(vals) - 2, -1, -1):
        idx = jnp.where(vals[j] == best, j, idx)
    return best, idx


def _router_kernel(x_ref, mod_ref, g_ref, wr_hi_ref, wr_lo_ref, br_ref, aux_ref):
    m = mod_ref[...]
    h = _rms_modulate(x_ref[...], g_ref[...], m[3:4], m[4:5])
    h_hi = h.astype(BF16)
    h_lo = (h - h_hi.astype(F32)).astype(BF16)
    logits = (_dot_nt(wr_hi_ref[...], h_hi) + _dot_nt(wr_hi_ref[...], h_lo)
              + _dot_nt(wr_lo_ref[...], h_hi)) + br_ref[...]
    row = lambda r: logits[r:r + 1, :]
    grp = [row(j) for j in range(N_EXPERT_GROUPS)]
    g_max, g_sel = _first_argmax(grp)
    g_w = 1.0 / functools.reduce(jnp.add, [jnp.exp(v - g_max) for v in grp])
    e_in = []
    for j in range(EXPERTS_PER_GROUP):
        v = row(N_EXPERT_GROUPS + (N_EXPERT_GROUPS - 1) * EXPERTS_PER_GROUP + j)
        for gi in range(N_EXPERT_GROUPS - 2, -1, -1):
            v = jnp.where(g_sel == gi, row(N_EXPERT_GROUPS + gi * EXPERTS_PER_GROUP + j), v)
        e_in.append(v)
    v1, i1 = _first_argmax(e_in)
    v2, i2 = _first_argmax([jnp.where(i1 == j, -jnp.inf, e_in[j])
                            for j in range(EXPERTS_PER_GROUP)])
    t = jnp.exp(v2 - v1)
    w1 = g_w / (1.0 + t)
    w2 = g_w * t / (1.0 + t)
    lo = jnp.minimum(i1, i2)
    hi = jnp.maximum(i1, i2)
    w_lo = jnp.where(i1 < i2, w1, w2)
    w_hi = jnp.where(i1 < i2, w2, w1)
    pair = jnp.where(lo == 0, hi - 1, jnp.where(lo == 1, hi + 1, 5))
    bucket = (g_sel * N_PAIRS + pair).astype(F32)
    pad = jnp.zeros((5, bucket.shape[1]), F32)
    aux_ref[...] = jnp.concatenate([bucket, w_lo, w_hi, pad], axis=0)


def _router(x, mods, layer, tiles_per_cond_row, g, wr_hi, wr_lo, br):
    n = x.shape[0]
    const = lambda i: (0, 0)
    tok = pl.BlockSpec((TOKEN_TILE, D_MODEL), lambda i: (i, 0))
    return pl.pallas_call(
        _router_kernel,
        grid=(n // TOKEN_TILE,),
        in_specs=[tok, _mod_spec(layer, tiles_per_cond_row),
                  pl.BlockSpec((1, D_MODEL), const),
                  pl.BlockSpec((ROUTER_ROWS, D_MODEL), const),
                  pl.BlockSpec((ROUTER_ROWS, D_MODEL), const),
                  pl.BlockSpec((ROUTER_ROWS, 1), const)],
        out_specs=pl.BlockSpec((8, TOKEN_TILE), lambda i: (0, i)),
        out_shape=jax.ShapeDtypeStruct((8, n), F32),
        compiler_params=_params(1),
        name="moe_router",
    )(x, mods, g, wr_hi, wr_lo, br)


ROW_W = D_MODEL + LANES


def _row_copies(idx_ref, base, hbm, buf, sem, n_rows, to_hbm):
    for r in range(n_rows):
        far = hbm.at[pl.ds(idx_ref[base + r], 1)]
        near = buf.at[pl.ds(r, 1)]
        src, dst = (near, far) if to_hbm else (far, near)
        pltpu.make_async_copy(src, dst, sem).start()


def _wait_rows(hbm, buf, sem):
    pltpu.make_async_copy(hbm.at[pl.ds(0, buf.shape[0])], buf, sem).wait()


def _dispatch_kernel(dest_ref, zero_at_ref, x_ref, mod_ref, g_ref, cw_ref, hs_hbm,
                     rowbuf, zbuf, sems, zsem):
    i = pl.program_id(0)
    last = pl.num_programs(0) - 1
    slot = i % 2

    @pl.when(i == 0)
    def _():
        zbuf[...] = jnp.zeros_like(zbuf)

        def clear(row0):
            return pltpu.make_async_copy(
                zbuf, hs_hbm.at[pl.ds(pl.multiple_of(row0, MOE_TILE), MOE_TILE)], zsem)

        n_tiles = hs_hbm.shape[0] // MOE_TILE
        n_used = zero_at_ref[N_BUCKETS]
        for b in range(N_BUCKETS):
            @pl.when(zero_at_ref[b] >= 0)
            def _():
                clear(zero_at_ref[b]).start()
        lax.fori_loop(n_used, n_tiles, lambda t, c: (clear(t * MOE_TILE).start(), c)[1], 0)
        for b in range(N_BUCKETS):
            @pl.when(zero_at_ref[b] >= 0)
            def _():
                clear(zero_at_ref[b]).wait()
        lax.fori_loop(n_used, n_tiles, lambda t, c: (clear(t * MOE_TILE).wait(), c)[1], 0)

    @pl.when(i >= 2)
    def _():
        _wait_rows(hs_hbm, rowbuf.at[slot], sems.at[slot])

    m = mod_ref[...]
    rowbuf[slot, :, :D_MODEL] = _rms_modulate(x_ref[...], g_ref[...], m[3:4], m[4:5])
    rowbuf[slot, :, D_MODEL:] = cw_ref[...]
    _row_copies(dest_ref, i * TOKEN_TILE, hs_hbm, rowbuf.at[slot], sems.at[slot],
                TOKEN_TILE, to_hbm=True)

    @pl.when(i == last)
    def _():
        @pl.when(i >= 1)
        def _():
            _wait_rows(hs_hbm, rowbuf.at[1 - slot], sems.at[1 - slot])
        _wait_rows(hs_hbm, rowbuf.at[slot], sems.at[slot])


def _dispatch(x, mods, layer, tiles_per_cond_row, g, dest, zero_at, cw_nat, n_pad):
    n = x.shape[0]
    tok = lambda w: pl.BlockSpec((TOKEN_TILE, w), lambda i, *_: (i, 0))
    return pl.pallas_call(
        _dispatch_kernel,
        grid_spec=pltpu.PrefetchScalarGridSpec(
            num_scalar_prefetch=2,
            grid=(n // TOKEN_TILE,),
            in_specs=[tok(D_MODEL), _mod_spec(layer, tiles_per_cond_row),
                      pl.BlockSpec((1, D_MODEL), lambda i, *_: (0, 0)), tok(LANES)],
            out_specs=pl.BlockSpec(memory_space=pl.ANY),
            scratch_shapes=[pltpu.VMEM((2, TOKEN_TILE, ROW_W), F32),
                            pltpu.VMEM((MOE_TILE, ROW_W), F32),
                            pltpu.SemaphoreType.DMA((2,)),
                            pltpu.SemaphoreType.DMA(())]),
        out_shape=jax.ShapeDtypeStruct((n_pad, ROW_W), F32),
        compiler_params=_params(1),
        name="moe_dispatch",
    )(dest, zero_at, x, mods, g, cw_nat)


def _expert_kernel(elo_ref, ehi_ref, nused_ref, hs_ref,
                   wg_lo, wu_lo, wd_lo, wg_hi, wu_hi, wd_hi, o_ref):
    t = pl.program_id(0)

    @pl.when(t < nused_ref[0])
    def _():
        x = hs_ref[:, :D_MODEL].astype(BF16)
        cw = hs_ref[:, D_MODEL:]
        y = None
        for e, (wg, wu, wd) in enumerate(((wg_lo, wu_lo, wd_lo), (wg_hi, wu_hi, wd_hi))):
            a = jnp.dot(x, wg[...], preferred_element_type=F32)
            u = jnp.dot(x, wu[...], preferred_element_type=F32)
            act = (a / (1.0 + jnp.exp(-a))) * u * cw[:, e:e + 1]
            part = jnp.dot(act.astype(BF16), wd[...], preferred_element_type=F32)
            y = part if y is None else y + part
        o_ref[...] = y

    @pl.when(t >= nused_ref[0])
    def _():
        o_ref[...] = jnp.zeros_like(o_ref)


def _experts(hs, e_lo, e_hi, n_used, wg, wu, wd):
    n_pad = hs.shape[0]
    row_in = lambda t, elo, ehi, nu: (jnp.minimum(t, nu[0] - 1), 0)
    row_out = lambda t, *_: (t, 0)
    w_lo = lambda t, elo, ehi, nu: (elo[t], 0, 0)
    w_hi = lambda t, elo, ehi, nu: (ehi[t], 0, 0)
    up = lambda f: pl.BlockSpec((None, D_MODEL, D_FF), f)
    down = lambda f: pl.BlockSpec((None, D_FF, D_MODEL), f)
    return pl.pallas_call(
        _expert_kernel,
        grid_spec=pltpu.PrefetchScalarGridSpec(
            num_scalar_prefetch=3,
            grid=(n_pad // MOE_TILE,),
            in_specs=[pl.BlockSpec((MOE_TILE, ROW_W), row_in),
                      up(w_lo), up(w_lo), down(w_lo), up(w_hi), up(w_hi), down(w_hi)],
            out_specs=pl.BlockSpec((MOE_TILE, D_MODEL), row_out)),
        out_shape=jax.ShapeDtypeStruct((n_pad, D_MODEL), F32),
        compiler_params=_params(1),
        name="moe_experts",
    )(e_lo, e_hi, n_used, hs, wg, wu, wd, wg, wu, wd)


def _unsort_kernel(dest_ref, x_ref, mod_ref, y_hbm, o_ref, ybuf, sems):
    i = pl.program_id(0)
    slot = i % 2

    def fetch(tile, s):
        _row_copies(dest_ref, tile * TOKEN_TILE, y_hbm, ybuf.at[s], sems.at[s],
                    TOKEN_TILE, to_hbm=False)

    @pl.when(i == 0)
    def _():
        fetch(0, 0)

    @pl.when(i + 1 < pl.num_programs(0))
    def _():
        fetch(i + 1, 1 - slot)

    _wait_rows(y_hbm, ybuf.at[slot], sems.at[slot])
    o_ref[...] = x_ref[...] + mod_ref[...][5:6] * ybuf[slot]


def _unsort_residual(x, mods, layer, tiles_per_cond_row, dest, y_sorted):
    n = x.shape[0]
    tok = pl.BlockSpec((TOKEN_TILE, D_MODEL), lambda i, *_: (i, 0))
    return pl.pallas_call(
        _unsort_kernel,
        grid_spec=pltpu.PrefetchScalarGridSpec(
            num_scalar_prefetch=1,
            grid=(n // TOKEN_TILE,),
            in_specs=[tok, _mod_spec(layer, tiles_per_cond_row),
                      pl.BlockSpec(memory_space=pl.ANY)],
            out_specs=tok,
            scratch_shapes=[pltpu.VMEM((2, TOKEN_TILE, D_MODEL), F32),
                            pltpu.SemaphoreType.DMA((2,))]),
        out_shape=jax.ShapeDtypeStruct(x.shape, F32),
        compiler_params=_params(1),
        name="moe_unsort_residual",
    )(dest, x, mods, y_sorted)


_PAIR_LO = (0, 0, 0, 1, 1, 2)
_PAIR_HI = (1, 2, 3, 2, 3, 3)


def _dispatch_plan(aux, n):
    n_tiles = n // MOE_TILE + N_BUCKETS
    bucket = aux[0].astype(jnp.int32)
    onehot = (bucket[:, None] == jnp.arange(N_BUCKETS)[None, :]).astype(jnp.int32)
    csum = jnp.cumsum(onehot, axis=0)
    counts = csum[-1]
    tiles = (counts + MOE_TILE - 1) // MOE_TILE
    tile_end = jnp.cumsum(tiles)
    tile_start = tile_end - tiles
    n_used = tile_end[-1]
    slot0 = tile_start * MOE_TILE - 1
    dest = jnp.sum(onehot * (csum + slot0[None, :]), axis=1).astype(jnp.int32)
    zero_at = jnp.where(tiles > 0, (tile_end - 1) * MOE_TILE, -1)
    zero_at = jnp.concatenate([zero_at, n_used[None]]).astype(jnp.int32)
    t = jnp.minimum(jnp.arange(n_tiles), n_used - 1)
    tile_bucket = jnp.sum((t[:, None] >= tile_end[None, :]).astype(jnp.int32), axis=1)
    grp = tile_bucket // N_PAIRS
    pair = tile_bucket % N_PAIRS
    e_lo = grp * EXPERTS_PER_GROUP + jnp.asarray(_PAIR_LO, jnp.int32)[pair]
    e_hi = grp * EXPERTS_PER_GROUP + jnp.asarray(_PAIR_HI, jnp.int32)[pair]
    cw_nat = jnp.pad(aux[1:3].T, ((0, 0), (0, LANES - 2)))
    return (dest, zero_at, e_lo.astype(jnp.int32), e_hi.astype(jnp.int32),
            n_used.astype(jnp.int32).reshape(1), cw_nat, n_tiles * MOE_TILE)


def _moe(x, mods, layer, tiles_per_cond_row, g, router_w, expert_w):
    aux = _router(x, mods, layer, tiles_per_cond_row, g, *router_w)
    dest, zero_at, e_lo, e_hi, n_used, cw_nat, n_pad = _dispatch_plan(aux, x.shape[0])
    hs = _dispatch(x, mods, layer, tiles_per_cond_row, g, dest, zero_at, cw_nat, n_pad)
    y_sorted = _experts(hs, e_lo, e_hi, n_used, *expert_w)
    return _unsort_residual(x, mods, layer, tiles_per_cond_row, dest, y_sorted)


def kernel(x_prompt, x_sample, cache_k, cache_v, c, c_ctx, norm1_g, norm2_g, w_mod, b_mod,
           w_qkv, q_norm_g, k_norm_g, rpb, w_o, w_pool, pool_scale, w_router_group,
           b_router_group, w_router_expert, b_router_expert, w_gate, w_up, w_down):
    bp, lp, _ = x_prompt.shape
    bs, ls, _ = x_sample.shape
    depth = w_mod.shape[0]
    assert bs + 1 <= COND_ROWS and ls % TOKEN_TILE == 0 and (bp * lp) % TOKEN_TILE == 0
    assert lp <= TOKEN_TILE and TOKEN_TILE % lp == 0

    cond = jnp.zeros((COND_ROWS, D_MODEL), F32).at[0].set(c_ctx).at[1:1 + bs].set(c)
    mods = _modulation(cond, w_mod, b_mod)

    xp = x_prompt.reshape(bp * lp, D_MODEL)
    xs = x_sample.reshape(bs * ls, D_MODEL)
    s_tiles = ls // TOKEN_TILE

    head_of = jnp.arange(MXU_DIM) // HEAD_DIM
    ones_bd = (head_of[:, None] == head_of[None, :]).astype(BF16)

    new_k, new_v = [], []
    for l in range(depth):
        j = l // 2
        g1 = norm1_g[l][None]
        if l % 2 == 0:
            w_bf16 = w_qkv[j].astype(BF16)
            wvt = w_bf16[:, 2 * D_MODEL:].T
            wo_bf16 = w_o[j].astype(BF16)
            qg = jnp.tile(q_norm_g[j], N_HEADS)[None]
            kg = jnp.tile(k_norm_g[j], N_HEADS)[None]
            scale = HEAD_DIM ** -0.5
            qp, kp, vp = _qkv(xp, mods, l, None, g1, w_bf16, None, qg, kg, ones_bd, F32, scale)
            new_k.append(kp.reshape(bp, lp, N_HEADS, HEAD_DIM))
            new_v.append(vp.reshape(bp, lp, N_HEADS, HEAD_DIM))
            xp = _ctx_attention(xp, mods, l, qp, kp, vp, wo_bf16, lp)
            qs, ks, vts = _qkv(xs, mods, l, s_tiles, g1, w_bf16, wvt, qg, kg, ones_bd, BF16,
                               scale * LOG2_E)
            kctx = cache_k[:, j].reshape(bs, -1, D_MODEL).astype(BF16)
            vctx_t = jnp.swapaxes(cache_v[:, j].reshape(bs, -1, D_MODEL), 1, 2).astype(BF16)
            xs = _neighbourhood_attention(xs, mods, l, qs, ks, vts, kctx, vctx_t,
                                          _na_bias_table(rpb[j]), wo_bf16, bs, ls)
        else:
            wp = w_pool[j].astype(BF16)
            ps = pool_scale[j][None]
            xp = _pool_mixer(xp, mods, l, None, g1, wp, ps, lp, lp)
            xs = _pool_mixer(xs, mods, l, s_tiles, g1, wp, ps, TOKEN_TILE, ls)

        wr = jnp.zeros((ROUTER_ROWS, D_MODEL), F32)
        wr = wr.at[:N_EXPERT_GROUPS].set(w_router_group[l].T)
        wr = wr.at[N_EXPERT_GROUPS:N_EXPERT_GROUPS + N_EXPERTS].set(w_router_expert[l].T)
        wr_hi = wr.astype(BF16)
        wr_lo = (wr - wr_hi.astype(F32)).astype(BF16)
        br = jnp.zeros((ROUTER_ROWS, 1), F32)
        br = br.at[:N_EXPERT_GROUPS, 0].set(b_router_group[l])
        br = br.at[N_EXPERT_GROUPS:N_EXPERT_GROUPS + N_EXPERTS, 0].set(b_router_expert[l])
        router_w = (wr_hi, wr_lo, br)
        expert_w = (w_gate[l].astype(BF16), w_up[l].astype(BF16), w_down[l].astype(BF16))
        g2 = norm2_g[l][None]
        xp = _moe(xp, mods, l, None, g2, router_w, expert_w)
        xs = _moe(xs, mods, l, s_tiles, g2, router_w, expert_w)

    return (xp.reshape(bp, lp, D_MODEL), xs.reshape(bs, ls, D_MODEL),
            jnp.stack(new_k, axis=1), jnp.stack(new_v, axis=1))
```

```python
import functools

import jax
import jax.numpy as jnp
import numpy as np
from jax import lax
from jax.experimental import pallas as pl
from jax.experimental.pallas import tpu as pltpu

D_MODEL = 1024
N_HEADS = 16
HEAD_DIM = D_MODEL // N_HEADS
GRID_W = 64
WIN_ROWS = 8
WIN_COLS = 16
POOL_SIZES = (2, 4, 8, 16)
POOL_GROUP_DIM = D_MODEL // len(POOL_SIZES)
POOL_HALO = 8
N_EXPERT_GROUPS = 4
EXPERTS_PER_GROUP = 4
N_EXPERTS = N_EXPERT_GROUPS * EXPERTS_PER_GROUP
N_PAIRS = 6
N_BUCKETS = N_EXPERT_GROUPS * N_PAIRS
D_FF = D_MODEL // 2
EPS = 1e-6
NEG = -1e30

LANES = 128
MXU_DIM = 256
TOKEN_TILE = 512
MOE_TILE = 256
ROUTER_ROWS = 32
COND_ROWS = 16
VMEM_LIMIT = 56 * 1024 * 1024

NA_ROWS = 4
NA_TOKENS = NA_ROWS * GRID_W
NA_SLOTS = 3 * NA_ROWS
BIAS_NONE = 2 * WIN_ROWS - 1
ONES_ROWS = 16
LOG2_E = 1.4426950408889634
ROW_W = D_MODEL + LANES

BF16 = jnp.bfloat16
F32 = jnp.float32


def _params(n_grid_dims, vmem=VMEM_LIMIT):
    return pltpu.CompilerParams(
        dimension_semantics=("arbitrary",) * n_grid_dims, vmem_limit_bytes=vmem)


def _rms_modulate(x, g, shift, scale):
    y = x * lax.rsqrt(jnp.mean(x * x, axis=-1, keepdims=True) + EPS)
    return (y * g) * (1.0 + scale) + shift


def _dot_nt(a, b):
    return lax.dot_general(a, b, (((1,), (1,)), ((), ())), preferred_element_type=F32)


def _mod_kernel(cond_ref, w_ref, b_ref, o_ref):
    c = cond_ref[...]
    s = c / (1.0 + jnp.exp(-c))
    o_ref[...] = jnp.dot(s, w_ref[...], preferred_element_type=F32,
                         precision=lax.Precision.HIGHEST) + b_ref[...]


def _modulation(cond, w_mod, b_mod):
    depth = w_mod.shape[0]
    tn = 1536
    out = pl.pallas_call(
        _mod_kernel,
        grid=(depth, 6 * D_MODEL // tn),
        in_specs=[
            pl.BlockSpec((COND_ROWS, D_MODEL), lambda l, n: (0, 0)),
            pl.BlockSpec((None, D_MODEL, tn), lambda l, n: (l, 0, n)),
            pl.BlockSpec((None, 1, tn), lambda l, n: (l, 0, n)),
        ],
        out_specs=pl.BlockSpec((None, COND_ROWS, tn), lambda l, n: (l, 0, n)),
        out_shape=jax.ShapeDtypeStruct((depth, COND_ROWS, 6 * D_MODEL), F32),
        compiler_params=_params(2),
        name="adaln_modulation",
    )(cond, w_mod, b_mod.reshape(depth, 1, 6 * D_MODEL))
    return out.reshape(depth, COND_ROWS, 6, D_MODEL)


def _mod_spec(layer, tiles_per_cond_row):
    if tiles_per_cond_row is None:
        return pl.BlockSpec((None, None, 6, D_MODEL), lambda i, *_: (layer, 0, 0, 0))
    return pl.BlockSpec((None, None, 6, D_MODEL),
                        lambda i, *_: (layer, 1 + i // tiles_per_cond_row, 0, 0))


def _wait_rows(hbm, buf, sem):
    pltpu.make_async_copy(hbm.at[pl.ds(0, buf.shape[0])], buf, sem).wait()


def _gathered_rows(idx_ref, y_hbm, ybuf, sems, index_of):
    i = pl.program_id(0)
    last = pl.num_programs(0) - 1
    slot = i % 2

    def fetch(step, s):
        for r in range(ybuf.shape[1]):
            pltpu.make_async_copy(y_hbm.at[pl.ds(idx_ref[index_of(step, r)], 1)],
                                  ybuf.at[s].at[pl.ds(r, 1)], sems.at[s]).start()

    @pl.when(i == 0)
    def _():
        fetch(0, 0)

    _wait_rows(y_hbm, ybuf.at[slot], sems.at[slot])
    y = ybuf[slot]
    fetch(jnp.minimum(i + 1, last), 1 - slot)

    def drain():
        @pl.when(i == last)
        def _():
            _wait_rows(y_hbm, ybuf.at[1 - slot], sems.at[1 - slot])

    return y, drain


def _pending_args(pending, mods, tiles_per_cond_row):
    dest, y_sorted, moe_layer = pending
    return ((dest,), (mods, y_sorted),
            [_mod_spec(moe_layer, tiles_per_cond_row), pl.BlockSpec(memory_space=pl.ANY)])


def _head_rms(t, ones_bd, gain):
    sq = (t * t).astype(BF16)
    parts = [jnp.dot(sq[:, c:c + MXU_DIM], ones_bd, preferred_element_type=F32)
             for c in range(0, D_MODEL, MXU_DIM)]
    mean = jnp.concatenate(parts, axis=-1) / HEAD_DIM
    return t * lax.rsqrt(mean + EPS) * gain


def _qkv_kernel(*refs, q_scale, v_transposed, pending):
    if pending:
        (dest_ref, x_ref, mod_ref, g_ref, wqk_ref, wv_ref, qg_ref, kg_ref, bd_ref,
         pmod_ref, y_hbm, q_ref, k_ref, v_ref, xnew_ref, ybuf, sems) = refs
        y, drain = _gathered_rows(dest_ref, y_hbm, ybuf, sems,
                                  lambda step, r: step * TOKEN_TILE + r)
        x = x_ref[...] + pmod_ref[...][5:6] * y
        xnew_ref[...] = x
    else:
        (x_ref, mod_ref, g_ref, wqk_ref, wv_ref, qg_ref, kg_ref, bd_ref,
         q_ref, k_ref, v_ref) = refs
        x, drain = x_ref[...], lambda: None
    m = mod_ref[...]
    h = _rms_modulate(x, g_ref[...], m[0:1], m[1:2]).astype(BF16)
    qk = jnp.dot(h, wqk_ref[...], preferred_element_type=F32)
    bd = bd_ref[...]
    q = _head_rms(qk[:, :D_MODEL], bd, qg_ref[...])
    k = _head_rms(qk[:, D_MODEL:], bd, kg_ref[...])
    q_ref[...] = (q * q_scale).astype(q_ref.dtype)
    k_ref[...] = k.astype(k_ref.dtype)
    if v_transposed:
        vt = _dot_nt(wv_ref[...], h).astype(v_ref.dtype)
        for j in range(TOKEN_TILE // NA_TOKENS):
            v_ref[j] = vt[:, j * NA_TOKENS:(j + 1) * NA_TOKENS]
    else:
        v_ref[...] = jnp.dot(h, wv_ref[...], preferred_element_type=F32).astype(v_ref.dtype)
    drain()


def _qkv(x, mods, layer, tiles_per_cond_row, g, w_bf16, wvt_bf16, qg, kg, bd, kv_dtype,
         q_scale, pending):
    n = x.shape[0]
    row = lambda i, *_: (i, 0)
    const = lambda i, *_: (0, 0)
    tok = pl.BlockSpec((TOKEN_TILE, D_MODEL), row)
    vec = pl.BlockSpec((1, D_MODEL), const)
    if wvt_bf16 is not None:
        per_tile = TOKEN_TILE // NA_TOKENS
        v_spec = pl.BlockSpec((per_tile, D_MODEL, NA_TOKENS), lambda i, *_: (i, 0, 0))
        v_shape = jax.ShapeDtypeStruct((n // NA_TOKENS, D_MODEL, NA_TOKENS), kv_dtype)
        wv, wv_spec = wvt_bf16, pl.BlockSpec((D_MODEL, D_MODEL), const)
    else:
        v_spec, v_shape = tok, jax.ShapeDtypeStruct((n, D_MODEL), kv_dtype)
        wv, wv_spec = w_bf16, pl.BlockSpec((D_MODEL, D_MODEL), lambda i, *_: (0, 2))
    in_specs = [tok, _mod_spec(layer, tiles_per_cond_row), vec,
                pl.BlockSpec((D_MODEL, 2 * D_MODEL), const), wv_spec, vec, vec,
                pl.BlockSpec((MXU_DIM, MXU_DIM), const)]
    out_specs = [tok, tok, v_spec]
    out_shape = [jax.ShapeDtypeStruct((n, D_MODEL), BF16),
                 jax.ShapeDtypeStruct((n, D_MODEL), kv_dtype), v_shape]
    prefetch, extra, scratch = (), (), []
    if pending is not None:
        prefetch, extra, extra_specs = _pending_args(pending, mods, tiles_per_cond_row)
        in_specs += extra_specs
        out_specs.append(tok)
        out_shape.append(jax.ShapeDtypeStruct((n, D_MODEL), F32))
        scratch = [pltpu.VMEM((2, TOKEN_TILE, D_MODEL), F32), pltpu.SemaphoreType.DMA((2,))]
    outs = pl.pallas_call(
        functools.partial(_qkv_kernel, q_scale=q_scale, v_transposed=wvt_bf16 is not None,
                          pending=pending is not None),
        grid_spec=pltpu.PrefetchScalarGridSpec(
            num_scalar_prefetch=len(prefetch), grid=(n // TOKEN_TILE,),
            in_specs=in_specs, out_specs=out_specs, scratch_shapes=scratch),
        out_shape=out_shape,
        compiler_params=_params(1),
        name="qkv_proj",
    )(*prefetch, x, mods, g, w_bf16, wv, qg, kg, bd, *extra)
    return (*outs[:3], outs[3] if pending is not None else x)


def _pair_attention(q2, score_fn, value_fn):
    lane = lax.broadcasted_iota(jnp.int32, q2.shape, 1)
    first = lane < HEAD_DIM
    outs = []
    for keep in (first, jnp.logical_not(first)):
        s = score_fn(jnp.where(keep, q2, jnp.zeros_like(q2)))
        m = functools.reduce(jnp.maximum, [jnp.max(b, axis=-1, keepdims=True) for b in s])
        p = [jnp.exp(b - m) for b in s]
        l = functools.reduce(jnp.add, [jnp.sum(b, axis=-1, keepdims=True) for b in p])
        outs.append(value_fn([b.astype(BF16) for b in p]) / l)
    return jnp.where(first, outs[0], outs[1])


def _ctx_attn_kernel(x_ref, mod_ref, q_ref, k_ref, v_ref, wo_ref, o_ref, att_ref):
    for hp in range(N_HEADS // 2):
        cols = slice(hp * LANES, (hp + 1) * LANES)
        k2 = k_ref[:, cols].astype(BF16)
        v2 = v_ref[:, cols].astype(BF16)
        att = _pair_attention(
            q_ref[:, cols],
            lambda q: [_dot_nt(q, k2)],
            lambda p: jnp.dot(p[0], v2, preferred_element_type=F32))
        att_ref[:, cols] = att.astype(BF16)
    y = jnp.dot(att_ref[...], wo_ref[...], preferred_element_type=F32)
    o_ref[...] = x_ref[...] + mod_ref[...][2:3] * y


def _ctx_attention(x, mods, layer, q, k, v, wo_bf16, seq):
    n = x.shape[0]
    row = lambda i: (i, 0)
    tok = pl.BlockSpec((seq, D_MODEL), row)
    return pl.pallas_call(
        _ctx_attn_kernel,
        grid=(n // seq,),
        in_specs=[tok, _mod_spec(layer, None), tok, tok, tok,
                  pl.BlockSpec((D_MODEL, D_MODEL), lambda i: (0, 0))],
        out_specs=tok,
        out_shape=jax.ShapeDtypeStruct((n, D_MODEL), F32),
        scratch_shapes=[pltpu.VMEM((seq, D_MODEL), BF16)],
        compiler_params=_params(1),
        name="context_attention",
    )(x, mods, q, k, v, wo_bf16)


def _na_kernel(x_ref, mod_ref, q_ref, kp_ref, kc_ref, kn_ref, vp_ref, vc_ref, vn_ref,
               kctx_ref, vctx_ref, bias_ref, wo_ref, o_ref, att_ref, *, n_grid_rows):
    rb = pl.program_id(1)
    entry = []
    for s in range(NA_SLOTS):
        kr = (rb - 1) * NA_ROWS + s
        per_row = []
        for i in range(NA_ROWS):
            r = rb * NA_ROWS + i
            rs = jnp.clip(r - WIN_ROWS // 2, 0, n_grid_rows - WIN_ROWS)
            in_band = jnp.logical_and(kr >= rs, kr < rs + WIN_ROWS)
            per_row.append(jnp.where(in_band, s - i + (WIN_ROWS - 1 - NA_ROWS), BIAS_NONE))
        entry.append(per_row)

    lane = lax.broadcasted_iota(jnp.int32, (GRID_W, LANES), 1)
    left = lane < GRID_W
    qlane = lax.broadcasted_iota(jnp.int32, (NA_TOKENS, LANES), 1)
    k_blocks = (kp_ref, kc_ref, kn_ref)
    v_blocks = (vp_ref, vc_ref, vn_ref)
    for hp in range(N_HEADS // 2):
        cols = slice(hp * LANES, (hp + 1) * LANES)
        q2 = q_ref[:, cols]
        halves = []
        for sub in range(2):
            head = 2 * hp + sub
            keep = (qlane < HEAD_DIM) if sub == 0 else (qlane >= HEAD_DIM)
            qm = jnp.where(keep, q2, jnp.zeros_like(q2))
            s_blocks = []
            for j in range(3):
                st = _dot_nt(k_blocks[j][:, cols], qm)
                rows = []
                for sr in range(NA_ROWS):
                    s = j * NA_ROWS + sr
                    bias = jnp.concatenate(
                        [jnp.where(left, bias_ref[head, entry[s][2 * t]],
                                   bias_ref[head, entry[s][2 * t + 1]])
                         for t in range(NA_ROWS // 2)], axis=-1)
                    rows.append(st[sr * GRID_W:(sr + 1) * GRID_W, :] + bias)
                s_blocks.append(jnp.concatenate(rows, axis=0))
            s_blocks.append(_dot_nt(kctx_ref[:, cols], qm))
            m = functools.reduce(jnp.maximum,
                                 [jnp.max(b, axis=0, keepdims=True) for b in s_blocks])
            p = [jnp.exp2(b - m).astype(BF16) for b in s_blocks]
            vrows = slice(hp * LANES + sub * HEAD_DIM, hp * LANES + (sub + 1) * HEAD_DIM)
            vt = [jnp.concatenate([vb[vrows, :], jnp.ones((ONES_ROWS, vb.shape[1]), BF16)],
                                  axis=0) for vb in v_blocks + (vctx_ref,)]
            ot = functools.reduce(jnp.add, [
                jnp.dot(vt[j], p[j], preferred_element_type=F32) for j in range(4)])
            halves.append(ot[:HEAD_DIM] / ot[HEAD_DIM:HEAD_DIM + 1])
        att_ref[:, cols] = jnp.concatenate(halves, axis=0).T.astype(BF16)
    y = jnp.dot(att_ref[...], wo_ref[...], preferred_element_type=F32)
    o_ref[...] = x_ref[...] + mod_ref[...][2:3] * y


def _neighbourhood_attention(x, mods, layer, q, k, vt, kctx, vctx_t, bias, wo_bf16, batch, seq):
    n_grid_rows = seq // GRID_W
    nrb = n_grid_rows // NA_ROWS
    cur = lambda b, r: (b * nrb + r, 0)
    prev = lambda b, r: (b * nrb + jnp.maximum(r - 1, 0), 0)
    nxt = lambda b, r: (b * nrb + jnp.minimum(r + 1, nrb - 1), 0)
    blk = lambda f: pl.BlockSpec((NA_TOKENS, D_MODEL), f)
    blk_t = lambda f: pl.BlockSpec((None, D_MODEL, NA_TOKENS),
                                   lambda b, r: (f(b, r)[0], 0, 0))
    n_ctx = kctx.shape[1]
    return pl.pallas_call(
        functools.partial(_na_kernel, n_grid_rows=n_grid_rows),
        grid=(batch, nrb),
        in_specs=[blk(cur),
                  pl.BlockSpec((None, None, 6, D_MODEL), lambda b, r: (layer, 1 + b, 0, 0)),
                  blk(cur), blk(prev), blk(cur), blk(nxt),
                  blk_t(prev), blk_t(cur), blk_t(nxt),
                  pl.BlockSpec((None, n_ctx, D_MODEL), lambda b, r: (b, 0, 0)),
                  pl.BlockSpec((None, D_MODEL, n_ctx), lambda b, r: (b, 0, 0)),
                  pl.BlockSpec(bias.shape, lambda b, r: (0, 0, 0, 0)),
                  pl.BlockSpec((D_MODEL, D_MODEL), lambda b, r: (0, 0))],
        out_specs=blk(cur),
        out_shape=jax.ShapeDtypeStruct(x.shape, F32),
        scratch_shapes=[pltpu.VMEM((NA_TOKENS, D_MODEL), BF16)],
        compiler_params=_params(2),
        name="neighbourhood_attention",
    )(x, mods, q, k, k, k, vt, vt, vt, kctx, vctx_t, bias, wo_bf16)


def _na_bias_table(rpb_layer):
    qc = np.arange(GRID_W)[None, :]
    kc = np.arange(GRID_W)[:, None]
    start = np.clip(qc - WIN_COLS // 2, 0, GRID_W - WIN_COLS)
    valid = (kc >= start) & (kc < start + WIN_COLS)
    off = kc - qc + WIN_COLS - 1
    select = ((off[None] == np.arange(2 * WIN_COLS - 1)[:, None, None]) & valid[None])
    m = jnp.einsum('hro,okq->hrkq', rpb_layer * LOG2_E, jnp.asarray(select, F32),
                   precision=lax.Precision.HIGHEST)
    m = m + jnp.asarray(np.where(valid, 0.0, NEG), F32)
    m = jnp.concatenate([m, jnp.full_like(m[:, :1], NEG)], axis=1)
    return jnp.concatenate([m, m], axis=-1)


def _pool_kernel(*refs, tile, seq, n_tokens, pending):
    if pending:
        (dest_ref, x_ref, xp_ref, xn_ref, mod_ref, g_ref, w_ref, ps_ref, pmod_ref, y_hbm,
         o_ref, h_ref, ybuf, sems) = refs

        def index_of(step, r):
            pos = step * tile + (r - POOL_HALO)
            inside = POOL_HALO <= r < POOL_HALO + tile
            return pos if inside else jnp.clip(pos, 0, n_tokens - 1)

        y, drain = _gathered_rows(dest_ref, y_hbm, ybuf, sems, index_of)
        gate = pmod_ref[...][5:6]
        x_prev = xp_ref[...] + gate * y[:POOL_HALO]
        x_cur = x_ref[...] + gate * y[POOL_HALO:POOL_HALO + tile]
        x_next = xn_ref[...] + gate * y[POOL_HALO + tile:]
    else:
        (x_ref, xp_ref, xn_ref, mod_ref, g_ref, w_ref, ps_ref, o_ref, h_ref) = refs
        x_prev, x_cur, x_next = xp_ref[...], x_ref[...], xn_ref[...]
        drain = lambda: None
    i = pl.program_id(0)
    tiles_per_seq = seq // tile
    t_in_seq = i % tiles_per_seq
    m = mod_ref[...]
    g = g_ref[...]
    h_cur = _rms_modulate(x_cur, g, m[0:1], m[1:2])
    h_prev = _rms_modulate(x_prev, g, m[0:1], m[1:2])
    h_next = _rms_modulate(x_next, g, m[0:1], m[1:2])
    h_ref[0:POOL_HALO, :] = jnp.where(t_in_seq > 0, h_prev, 0.0)
    h_ref[POOL_HALO:POOL_HALO + tile, :] = h_cur
    h_ref[POOL_HALO + tile:, :] = jnp.where(t_in_seq < tiles_per_seq - 1, h_next, 0.0)

    pos = t_in_seq * tile + lax.broadcasted_iota(jnp.int32, (tile, 1), 0)
    ys = []
    for grp, w in enumerate(POOL_SIZES):
        cols = slice(grp * POOL_GROUP_DIM, (grp + 1) * POOL_GROUP_DIM)
        total = h_ref[POOL_HALO - w // 2:POOL_HALO - w // 2 + tile, cols]
        for d in range(1 - w // 2, w - w // 2):
            total = total + h_ref[POOL_HALO + d:POOL_HALO + d + tile, cols]
        lo = jnp.clip(pos - w // 2, 0, seq)
        hi = jnp.clip(pos - w // 2 + w, 0, seq)
        pooled = total / (hi - lo).astype(F32)
        diff = (pooled - h_cur[:, cols]).astype(BF16)
        ys.append(jnp.dot(diff, w_ref[grp], preferred_element_type=F32))
    y_mix = jnp.concatenate(ys, axis=-1) * ps_ref[...]
    o_ref[...] = x_cur + m[2:3] * y_mix
    drain()


def _pool_mixer(x, mods, layer, tiles_per_cond_row, g, w_pool_bf16, pool_scale, tile, seq,
                pending):
    n = x.shape[0]
    hb = tile // POOL_HALO
    last = n // POOL_HALO - 1
    const = lambda i, *_: (0, 0)
    in_specs = [pl.BlockSpec((tile, D_MODEL), lambda i, *_: (i, 0)),
                pl.BlockSpec((POOL_HALO, D_MODEL),
                             lambda i, *_: (jnp.maximum(i * hb - 1, 0), 0)),
                pl.BlockSpec((POOL_HALO, D_MODEL),
                             lambda i, *_: (jnp.minimum((i + 1) * hb, last), 0)),
                _mod_spec(layer, tiles_per_cond_row),
                pl.BlockSpec((1, D_MODEL), const),
                pl.BlockSpec(w_pool_bf16.shape, lambda i, *_: (0, 0, 0)),
                pl.BlockSpec((1, D_MODEL), const)]
    scratch = [pltpu.VMEM((tile + 2 * POOL_HALO, D_MODEL), F32)]
    prefetch, extra = (), ()
    if pending is not None:
        prefetch, extra, extra_specs = _pending_args(pending, mods, tiles_per_cond_row)
        in_specs += extra_specs
        scratch += [pltpu.VMEM((2, tile + 2 * POOL_HALO, D_MODEL), F32),
                    pltpu.SemaphoreType.DMA((2,))]
    return pl.pallas_call(
        functools.partial(_pool_kernel, tile=tile, seq=seq, n_tokens=n,
                          pending=pending is not None),
        grid_spec=pltpu.PrefetchScalarGridSpec(
            num_scalar_prefetch=len(prefetch), grid=(n // tile,),
            in_specs=in_specs,
            out_specs=pl.BlockSpec((tile, D_MODEL), lambda i, *_: (i, 0)),
            scratch_shapes=scratch),
        out_shape=jax.ShapeDtypeStruct(x.shape, F32),
        compiler_params=_params(1),
        name="pool_mixer",
    )(*prefetch, x, x, x, mods, g, w_pool_bf16, pool_scale, *extra)


def _first_argmax(vals):
    best = functools.reduce(jnp.maximum, vals)
    idx = jnp.full(best.shape, len(vals) - 1, jnp.int32)
    for j in range(len(vals) - 2, -1, -1):
        idx = jnp.where(vals[j] == best, j, idx)
    return best, idx


def _router_kernel(x_ref, mod_ref, g_ref, wr_hi_ref, wr_lo_ref, br_ref, aux_ref):
    m = mod_ref[...]
    h = _rms_modulate(x_ref[...], g_ref[...], m[3:4], m[4:5])
    h_hi = h.astype(BF16)
    h_lo = (h - h_hi.astype(F32)).astype(BF16)
    w_hi = wr_hi_ref[...]
    both = _dot_nt(jnp.concatenate([w_hi, wr_lo_ref[...]], axis=0), h_hi)
    logits = (both[:ROUTER_ROWS] + _dot_nt(w_hi, h_lo) + both[ROUTER_ROWS:]) + br_ref[...]
    row = lambda r: logits[r:r + 1, :]
    grp = [row(j) for j in range(N_EXPERT_GROUPS)]
    g_max, g_sel = _first_argmax(grp)
    g_w = 1.0 / functools.reduce(jnp.add, [jnp.exp(v - g_max) for v in grp])
    e_in = []
    for j in range(EXPERTS_PER_GROUP):
        v = row(N_EXPERT_GROUPS + (N_EXPERT_GROUPS - 1) * EXPERTS_PER_GROUP + j)
        for gi in range(N_EXPERT_GROUPS - 2, -1, -1):
            v = jnp.where(g_sel == gi, row(N_EXPERT_GROUPS + gi * EXPERTS_PER_GROUP + j), v)
        e_in.append(v)
    v1, i1 = _first_argmax(e_in)
    v2, i2 = _first_argmax([jnp.where(i1 == j, -jnp.inf, e_in[j])
                            for j in range(EXPERTS_PER_GROUP)])
    t = jnp.exp(v2 - v1)
    w1 = g_w / (1.0 + t)
    w2 = g_w * t / (1.0 + t)
    lo = jnp.minimum(i1, i2)
    hi = jnp.maximum(i1, i2)
    w_lo = jnp.where(i1 < i2, w1, w2)
    w_hi = jnp.where(i1 < i2, w2, w1)
    pair = jnp.where(lo == 0, hi - 1, jnp.where(lo == 1, hi + 1, 5))
    bucket = (g_sel * N_PAIRS + pair).astype(F32)
    pad = jnp.zeros((5, bucket.shape[1]), F32)
    aux_ref[...] = jnp.concatenate([bucket, w_lo, w_hi, pad], axis=0)


def _router(x, mods, layer, tiles_per_cond_row, g, wr_hi, wr_lo, br):
    n = x.shape[0]
    const = lambda i: (0, 0)
    tok = pl.BlockSpec((TOKEN_TILE, D_MODEL), lambda i: (i, 0))
    return pl.pallas_call(
        _router_kernel,
        grid=(n // TOKEN_TILE,),
        in_specs=[tok, _mod_spec(layer, tiles_per_cond_row),
                  pl.BlockSpec((1, D_MODEL), const),
                  pl.BlockSpec((ROUTER_ROWS, D_MODEL), const),
                  pl.BlockSpec((ROUTER_ROWS, D_MODEL), const),
                  pl.BlockSpec((ROUTER_ROWS, 1), const)],
        out_specs=pl.BlockSpec((8, TOKEN_TILE), lambda i: (0, i)),
        out_shape=jax.ShapeDtypeStruct((8, n), F32),
        compiler_params=_params(1),
        name="moe_router",
    )(x, mods, g, wr_hi, wr_lo, br)


def _dispatch_kernel(dest_ref, zero_at_ref, x_ref, mod_ref, g_ref, cw_ref, hs_hbm,
                     rowbuf, zbuf, sems, zsem):
    i = pl.program_id(0)
    last = pl.num_programs(0) - 1
    slot = i % 2

    @pl.when(i == 0)
    def _():
        zbuf[...] = jnp.zeros_like(zbuf)

        def clear(row0):
            return pltpu.make_async_copy(
                zbuf, hs_hbm.at[pl.ds(pl.multiple_of(row0, MOE_TILE), MOE_TILE)], zsem)

        n_tiles = hs_hbm.shape[0] // MOE_TILE
        n_used = zero_at_ref[N_BUCKETS]
        for b in range(N_BUCKETS):
            @pl.when(zero_at_ref[b] >= 0)
            def _():
                clear(zero_at_ref[b]).start()
        lax.fori_loop(n_used, n_tiles, lambda t, c: (clear(t * MOE_TILE).start(), c)[1], 0)
        for b in range(N_BUCKETS):
            @pl.when(zero_at_ref[b] >= 0)
            def _():
                clear(zero_at_ref[b]).wait()
        lax.fori_loop(n_used, n_tiles, lambda t, c: (clear(t * MOE_TILE).wait(), c)[1], 0)

    @pl.when(i >= 2)
    def _():
        _wait_rows(hs_hbm, rowbuf.at[slot], sems.at[slot])

    m = mod_ref[...]
    rowbuf[slot, :, :D_MODEL] = _rms_modulate(x_ref[...], g_ref[...], m[3:4], m[4:5])
    rowbuf[slot, :, D_MODEL:] = cw_ref[...]
    for r in range(TOKEN_TILE):
        pltpu.make_async_copy(rowbuf.at[slot].at[pl.ds(r, 1)],
                              hs_hbm.at[pl.ds(dest_ref[i * TOKEN_TILE + r], 1)],
                              sems.at[slot]).start()

    @pl.when(i == last)
    def _():
        @pl.when(i >= 1)
        def _():
            _wait_rows(hs_hbm, rowbuf.at[1 - slot], sems.at[1 - slot])
        _wait_rows(hs_hbm, rowbuf.at[slot], sems.at[slot])


def _dispatch(x, mods, layer, tiles_per_cond_row, g, dest, zero_at, cw_nat, n_pad):
    n = x.shape[0]
    tok = lambda w: pl.BlockSpec((TOKEN_TILE, w), lambda i, *_: (i, 0))
    return pl.pallas_call(
        _dispatch_kernel,
        grid_spec=pltpu.PrefetchScalarGridSpec(
            num_scalar_prefetch=2,
            grid=(n // TOKEN_TILE,),
            in_specs=[tok(D_MODEL), _mod_spec(layer, tiles_per_cond_row),
                      pl.BlockSpec((1, D_MODEL), lambda i, *_: (0, 0)), tok(LANES)],
            out_specs=pl.BlockSpec(memory_space=pl.ANY),
            scratch_shapes=[pltpu.VMEM((2, TOKEN_TILE, ROW_W), F32),
                            pltpu.VMEM((MOE_TILE, ROW_W), F32),
                            pltpu.SemaphoreType.DMA((2,)),
                            pltpu.SemaphoreType.DMA(())]),
        out_shape=jax.ShapeDtypeStruct((n_pad, ROW_W), F32),
        compiler_params=_params(1),
        name="moe_dispatch",
    )(dest, zero_at, x, mods, g, cw_nat)


def _expert_kernel(elo_ref, ehi_ref, nused_ref, hs_ref,
                   wg_lo, wu_lo, wd_lo, wg_hi, wu_hi, wd_hi, o_ref):
    t = pl.program_id(0)

    @pl.when(t < nused_ref[0])
    def _():
        x = hs_ref[:, :D_MODEL].astype(BF16)
        cw = hs_ref[:, D_MODEL:]
        y = None
        for e, (wg, wu, wd) in enumerate(((wg_lo, wu_lo, wd_lo), (wg_hi, wu_hi, wd_hi))):
            a = jnp.dot(x, wg[...], preferred_element_type=F32)
            u = jnp.dot(x, wu[...], preferred_element_type=F32)
            act = (a / (1.0 + jnp.exp(-a))) * u * cw[:, e:e + 1]
            part = jnp.dot(act.astype(BF16), wd[...], preferred_element_type=F32)
            y = part if y is None else y + part
        o_ref[...] = y

    @pl.when(t >= nused_ref[0])
    def _():
        o_ref[...] = jnp.zeros_like(o_ref)


def _experts(hs, e_lo, e_hi, n_used, layer, wg, wu, wd):
    n_pad = hs.shape[0]
    row_in = lambda t, elo, ehi, nu: (jnp.minimum(t, nu[0] - 1), 0)
    row_out = lambda t, *_: (t, 0)
    w_lo = lambda t, elo, ehi, nu: (layer, elo[t], 0, 0)
    w_hi = lambda t, elo, ehi, nu: (layer, ehi[t], 0, 0)
    up = lambda f: pl.BlockSpec((None, None, D_MODEL, D_FF), f)
    down = lambda f: pl.BlockSpec((None, None, D_FF, D_MODEL), f)
    return pl.pallas_call(
        _expert_kernel,
        grid_spec=pltpu.PrefetchScalarGridSpec(
            num_scalar_prefetch=3,
            grid=(n_pad // MOE_TILE,),
            in_specs=[pl.BlockSpec((MOE_TILE, ROW_W), row_in),
                      up(w_lo), up(w_lo), down(w_lo), up(w_hi), up(w_hi), down(w_hi)],
            out_specs=pl.BlockSpec((MOE_TILE, D_MODEL), row_out)),
        out_shape=jax.ShapeDtypeStruct((n_pad, D_MODEL), F32),
        compiler_params=_params(1),
        name="moe_experts",
    )(e_lo, e_hi, n_used, hs, wg, wu, wd, wg, wu, wd)


def _unsort_kernel(dest_ref, x_ref, mod_ref, y_hbm, o_ref, ybuf, sems):
    y, drain = _gathered_rows(dest_ref, y_hbm, ybuf, sems,
                              lambda step, r: step * TOKEN_TILE + r)
    o_ref[...] = x_ref[...] + mod_ref[...][5:6] * y
    drain()


def _unsort_residual(x, mods, tiles_per_cond_row, pending):
    n = x.shape[0]
    tok = pl.BlockSpec((TOKEN_TILE, D_MODEL), lambda i, *_: (i, 0))
    prefetch, extra, extra_specs = _pending_args(pending, mods, tiles_per_cond_row)
    return pl.pallas_call(
        _unsort_kernel,
        grid_spec=pltpu.PrefetchScalarGridSpec(
            num_scalar_prefetch=len(prefetch),
            grid=(n // TOKEN_TILE,),
            in_specs=[tok] + extra_specs,
            out_specs=tok,
            scratch_shapes=[pltpu.VMEM((2, TOKEN_TILE, D_MODEL), F32),
                            pltpu.SemaphoreType.DMA((2,))]),
        out_shape=jax.ShapeDtypeStruct(x.shape, F32),
        compiler_params=_params(1),
        name="moe_unsort_residual",
    )(*prefetch, x, *extra)


_PAIR_LO = (0, 0, 0, 1, 1, 2)
_PAIR_HI = (1, 2, 3, 2, 3, 3)


def _dispatch_plan(aux, n):
    n_tiles = n // MOE_TILE + N_BUCKETS
    bucket = aux[0].astype(jnp.int32)
    onehot = (bucket[:, None] == jnp.arange(N_BUCKETS)[None, :]).astype(jnp.int32)
    csum = jnp.cumsum(onehot, axis=0)
    counts = csum[-1]
    tiles = (counts + MOE_TILE - 1) // MOE_TILE
    tile_end = jnp.cumsum(tiles)
    tile_start = tile_end - tiles
    n_used = tile_end[-1]
    slot0 = tile_start * MOE_TILE - 1
    dest = jnp.sum(onehot * (csum + slot0[None, :]), axis=1).astype(jnp.int32)
    zero_at = jnp.where(tiles > 0, (tile_end - 1) * MOE_TILE, -1)
    zero_at = jnp.concatenate([zero_at, n_used[None]]).astype(jnp.int32)
    t = jnp.minimum(jnp.arange(n_tiles), n_used - 1)
    tile_bucket = jnp.sum((t[:, None] >= tile_end[None, :]).astype(jnp.int32), axis=1)
    grp = tile_bucket // N_PAIRS
    pair = tile_bucket % N_PAIRS
    e_lo = grp * EXPERTS_PER_GROUP + jnp.asarray(_PAIR_LO, jnp.int32)[pair]
    e_hi = grp * EXPERTS_PER_GROUP + jnp.asarray(_PAIR_HI, jnp.int32)[pair]
    cw_nat = jnp.pad(aux[1:3].T, ((0, 0), (0, LANES - 2)))
    return (dest, zero_at, e_lo.astype(jnp.int32), e_hi.astype(jnp.int32),
            n_used.astype(jnp.int32).reshape(1), cw_nat, n_tiles * MOE_TILE)


def _moe(x, mods, layer, tiles_per_cond_row, g, router_w, expert_w):
    aux = _router(x, mods, layer, tiles_per_cond_row, g, *router_w)
    dest, zero_at, e_lo, e_hi, n_used, cw_nat, n_pad = _dispatch_plan(aux, x.shape[0])
    hs = _dispatch(x, mods, layer, tiles_per_cond_row, g, dest, zero_at, cw_nat, n_pad)
    y_sorted = _experts(hs, e_lo, e_hi, n_used, layer, *expert_w)
    return dest, y_sorted, layer


def kernel(x_prompt, x_sample, cache_k, cache_v, c, c_ctx, norm1_g, norm2_g, w_mod, b_mod,
           w_qkv, q_norm_g, k_norm_g, rpb, w_o, w_pool, pool_scale, w_router_group,
           b_router_group, w_router_expert, b_router_expert, w_gate, w_up, w_down):
    bp, lp, _ = x_prompt.shape
    bs, ls, _ = x_sample.shape
    depth = w_mod.shape[0]
    assert bs + 1 <= COND_ROWS and ls % TOKEN_TILE == 0 and (bp * lp) % TOKEN_TILE == 0
    assert lp <= TOKEN_TILE and TOKEN_TILE % lp == 0

    cond = jnp.zeros((COND_ROWS, D_MODEL), F32).at[0].set(c_ctx).at[1:1 + bs].set(c)
    mods = _modulation(cond, w_mod, b_mod)

    xp = x_prompt.reshape(bp * lp, D_MODEL)
    xs = x_sample.reshape(bs * ls, D_MODEL)
    s_tiles = ls // TOKEN_TILE

    head_of = jnp.arange(MXU_DIM) // HEAD_DIM
    ones_bd = (head_of[:, None] == head_of[None, :]).astype(BF16)

    expert_w = (w_gate.astype(BF16), w_up.astype(BF16), w_down.astype(BF16))
    new_k, new_v = [], []
    pend_p = pend_s = None
    for l in range(depth):
        j = l // 2
        g1 = norm1_g[l][None]
        if l % 2 == 0:
            w_bf16 = w_qkv[j].astype(BF16)
            wvt = w_bf16[:, 2 * D_MODEL:].T
            wo_bf16 = w_o[j].astype(BF16)
            qg = jnp.tile(q_norm_g[j], N_HEADS)[None]
            kg = jnp.tile(k_norm_g[j], N_HEADS)[None]
            scale = HEAD_DIM ** -0.5
            qp, kp, vp, xp = _qkv(xp, mods, l, None, g1, w_bf16, None, qg, kg, ones_bd, F32,
                                  scale, pend_p)
            new_k.append(kp.reshape(bp, lp, N_HEADS, HEAD_DIM))
            new_v.append(vp.reshape(bp, lp, N_HEADS, HEAD_DIM))
            xp = _ctx_attention(xp, mods, l, qp, kp, vp, wo_bf16, lp)
            qs, ks, vts, xs = _qkv(xs, mods, l, s_tiles, g1, w_bf16, wvt, qg, kg, ones_bd,
                                   BF16, scale * LOG2_E, pend_s)
            kctx = cache_k[:, j].reshape(bs, -1, D_MODEL).astype(BF16)
            vctx_t = jnp.swapaxes(cache_v[:, j].reshape(bs, -1, D_MODEL), 1, 2).astype(BF16)
            xs = _neighbourhood_attention(xs, mods, l, qs, ks, vts, kctx, vctx_t,
                                          _na_bias_table(rpb[j]), wo_bf16, bs, ls)
        else:
            wp = w_pool[j].astype(BF16)
            ps = pool_scale[j][None]
            xp = _pool_mixer(xp, mods, l, None, g1, wp, ps, lp, lp, pend_p)
            xs = _pool_mixer(xs, mods, l, s_tiles, g1, wp, ps, TOKEN_TILE, ls, pend_s)

        wr = jnp.zeros((ROUTER_ROWS, D_MODEL), F32)
        wr = wr.at[:N_EXPERT_GROUPS].set(w_router_group[l].T)
        wr = wr.at[N_EXPERT_GROUPS:N_EXPERT_GROUPS + N_EXPERTS].set(w_router_expert[l].T)
        wr_hi = wr.astype(BF16)
        wr_lo = (wr - wr_hi.astype(F32)).astype(BF16)
        br = jnp.zeros((ROUTER_ROWS, 1), F32)
        br = br.at[:N_EXPERT_GROUPS, 0].set(b_router_group[l])
        br = br.at[N_EXPERT_GROUPS:N_EXPERT_GROUPS + N_EXPERTS, 0].set(b_router_expert[l])
        router_w = (wr_hi, wr_lo, br)
        g2 = norm2_g[l][None]
        pend_p = _moe(xp, mods, l, None, g2, router_w, expert_w)
        pend_s = _moe(xs, mods, l, s_tiles, g2, router_w, expert_w)

    xp = _unsort_residual(xp, mods, None, pend_p)
    xs = _unsort_residual(xs, mods, s_tiles, pend_s)
    return (xp.reshape(bp, lp, D_MODEL), xs.reshape(bs, ls, D_MODEL),
            jnp.stack(new_k, axis=1), jnp.stack(new_v, axis=1))
```

```python
import functools

import jax
import jax.numpy as jnp
import numpy as np
from jax import lax
from jax.experimental import pallas as pl
from jax.experimental.pallas import tpu as pltpu

D_MODEL = 1024
N_HEADS = 16
HEAD_DIM = D_MODEL // N_HEADS
GRID_W = 64
WIN_ROWS = 8
WIN_COLS = 16
POOL_SIZES = (2, 4, 8, 16)
POOL_GROUP_DIM = D_MODEL // len(POOL_SIZES)
POOL_HALO = 8
N_EXPERT_GROUPS = 4
EXPERTS_PER_GROUP = 4
N_EXPERTS = N_EXPERT_GROUPS * EXPERTS_PER_GROUP
N_PAIRS = 6
N_BUCKETS = N_EXPERT_GROUPS * N_PAIRS
D_FF = D_MODEL // 2
EPS = 1e-6
NEG = -1e30

LANES = 128
MXU_DIM = 256
TOKEN_TILE = 512
MOE_TILE = 256
ROUTER_ROWS = 32
COND_ROWS = 16
VMEM_LIMIT = 56 * 1024 * 1024

NA_ROWS = 4
NA_TOKENS = NA_ROWS * GRID_W
NA_SLOTS = 3 * NA_ROWS
BIAS_NONE = 2 * WIN_ROWS - 1
ONES_ROWS = 16
LOG2_E = 1.4426950408889634
ROW_W = D_MODEL + LANES
GATHER_SLOTS = 3

BF16 = jnp.bfloat16
F32 = jnp.float32


def _params(n_grid_dims, vmem=VMEM_LIMIT):
    return pltpu.CompilerParams(
        dimension_semantics=("arbitrary",) * n_grid_dims, vmem_limit_bytes=vmem)


def _rms_modulate(x, g, shift, scale):
    y = x * lax.rsqrt(jnp.mean(x * x, axis=-1, keepdims=True) + EPS)
    return (y * g) * (1.0 + scale) + shift


def _dot_nt(a, b):
    return lax.dot_general(a, b, (((1,), (1,)), ((), ())), preferred_element_type=F32)


def _mod_kernel(cond_ref, w_ref, b_ref, o_ref):
    c = cond_ref[...]
    s = c / (1.0 + jnp.exp(-c))
    o_ref[...] = jnp.dot(s, w_ref[...], preferred_element_type=F32,
                         precision=lax.Precision.HIGHEST) + b_ref[...]


def _modulation(cond, w_mod, b_mod):
    depth = w_mod.shape[0]
    tn = 1536
    out = pl.pallas_call(
        _mod_kernel,
        grid=(depth, 6 * D_MODEL // tn),
        in_specs=[
            pl.BlockSpec((COND_ROWS, D_MODEL), lambda l, n: (0, 0)),
            pl.BlockSpec((None, D_MODEL, tn), lambda l, n: (l, 0, n)),
            pl.BlockSpec((None, 1, tn), lambda l, n: (l, 0, n)),
        ],
        out_specs=pl.BlockSpec((None, COND_ROWS, tn), lambda l, n: (l, 0, n)),
        out_shape=jax.ShapeDtypeStruct((depth, COND_ROWS, 6 * D_MODEL), F32),
        compiler_params=_params(2),
        name="adaln_modulation",
    )(cond, w_mod, b_mod.reshape(depth, 1, 6 * D_MODEL))
    return out.reshape(depth, COND_ROWS, 6, D_MODEL)


def _mod_spec(layer, tiles_per_cond_row):
    if tiles_per_cond_row is None:
        return pl.BlockSpec((None, None, 6, D_MODEL), lambda i, *_: (layer, 0, 0, 0))
    return pl.BlockSpec((None, None, 6, D_MODEL),
                        lambda i, *_: (layer, 1 + i // tiles_per_cond_row, 0, 0))


def _wait_rows(hbm, buf, sem):
    pltpu.make_async_copy(hbm.at[pl.ds(0, buf.shape[0])], buf, sem).wait()


def _gathered_rows(idx_ref, y_hbm, ybuf, sems, index_of):
    i = pl.program_id(0)
    last = pl.num_programs(0) - 1
    slot = i % GATHER_SLOTS

    def fetch(step, s):
        for r in range(ybuf.shape[1]):
            pltpu.make_async_copy(y_hbm.at[pl.ds(idx_ref[index_of(step, r)], 1)],
                                  ybuf.at[s].at[pl.ds(r, 1)], sems.at[s]).start()

    @pl.when(i == 0)
    def _():
        fetch(0, 0)
        fetch(jnp.minimum(1, last), 1)

    _wait_rows(y_hbm, ybuf.at[slot], sems.at[slot])
    y = ybuf[slot]
    ahead = (i + 2) % GATHER_SLOTS
    fetch(jnp.minimum(i + 2, last), ahead)

    def drain():
        @pl.when(i == last)
        def _():
            for s in (ahead, (i + 1) % GATHER_SLOTS):
                _wait_rows(y_hbm, ybuf.at[s], sems.at[s])

    return y, drain


def _pending_args(pending, mods, tiles_per_cond_row):
    dest, y_sorted, moe_layer = pending
    return ((dest,), (mods, y_sorted),
            [_mod_spec(moe_layer, tiles_per_cond_row), pl.BlockSpec(memory_space=pl.ANY)])


def _head_rms(t, ones_bd, gain):
    sq = (t * t).astype(BF16)
    parts = [jnp.dot(sq[:, c:c + MXU_DIM], ones_bd, preferred_element_type=F32)
             for c in range(0, D_MODEL, MXU_DIM)]
    mean = jnp.concatenate(parts, axis=-1) / HEAD_DIM
    return t * lax.rsqrt(mean + EPS) * gain


def _qkv_kernel(*refs, q_scale, v_transposed, pending):
    if pending:
        (dest_ref, x_ref, mod_ref, g_ref, wqk_ref, wv_ref, qg_ref, kg_ref, bd_ref,
         pmod_ref, y_hbm, q_ref, k_ref, v_ref, xnew_ref, ybuf, sems) = refs
        y, drain = _gathered_rows(dest_ref, y_hbm, ybuf, sems,
                                  lambda step, r: step * TOKEN_TILE + r)
        x = x_ref[...] + pmod_ref[...][5:6] * y
        xnew_ref[...] = x
    else:
        (x_ref, mod_ref, g_ref, wqk_ref, wv_ref, qg_ref, kg_ref, bd_ref,
         q_ref, k_ref, v_ref) = refs
        x, drain = x_ref[...], lambda: None
    m = mod_ref[...]
    h = _rms_modulate(x, g_ref[...], m[0:1], m[1:2]).astype(BF16)
    qk = jnp.dot(h, wqk_ref[...], preferred_element_type=F32)
    bd = bd_ref[...]
    q = _head_rms(qk[:, :D_MODEL], bd, qg_ref[...])
    k = _head_rms(qk[:, D_MODEL:], bd, kg_ref[...])
    q_ref[...] = (q * q_scale).astype(q_ref.dtype)
    k_ref[...] = k.astype(k_ref.dtype)
    if v_transposed:
        vt = _dot_nt(wv_ref[...], h).astype(v_ref.dtype)
        for j in range(TOKEN_TILE // NA_TOKENS):
            v_ref[j] = vt[:, j * NA_TOKENS:(j + 1) * NA_TOKENS]
    else:
        v_ref[...] = jnp.dot(h, wv_ref[...], preferred_element_type=F32).astype(v_ref.dtype)
    drain()


def _qkv(x, mods, layer, tiles_per_cond_row, g, w_bf16, wvt_bf16, qg, kg, bd, kv_dtype,
         q_scale, pending):
    n = x.shape[0]
    row = lambda i, *_: (i, 0)
    const = lambda i, *_: (0, 0)
    tok = pl.BlockSpec((TOKEN_TILE, D_MODEL), row)
    vec = pl.BlockSpec((1, D_MODEL), const)
    if wvt_bf16 is not None:
        per_tile = TOKEN_TILE // NA_TOKENS
        v_spec = pl.BlockSpec((per_tile, D_MODEL, NA_TOKENS), lambda i, *_: (i, 0, 0))
        v_shape = jax.ShapeDtypeStruct((n // NA_TOKENS, D_MODEL, NA_TOKENS), kv_dtype)
        wv, wv_spec = wvt_bf16, pl.BlockSpec((D_MODEL, D_MODEL), const)
    else:
        v_spec, v_shape = tok, jax.ShapeDtypeStruct((n, D_MODEL), kv_dtype)
        wv, wv_spec = w_bf16, pl.BlockSpec((D_MODEL, D_MODEL), lambda i, *_: (0, 2))
    in_specs = [tok, _mod_spec(layer, tiles_per_cond_row), vec,
                pl.BlockSpec((D_MODEL, 2 * D_MODEL), const), wv_spec, vec, vec,
                pl.BlockSpec((MXU_DIM, MXU_DIM), const)]
    out_specs = [tok, tok, v_spec]
    out_shape = [jax.ShapeDtypeStruct((n, D_MODEL), BF16),
                 jax.ShapeDtypeStruct((n, D_MODEL), kv_dtype), v_shape]
    prefetch, extra, scratch = (), (), []
    if pending is not None:
        prefetch, extra, extra_specs = _pending_args(pending, mods, tiles_per_cond_row)
        in_specs += extra_specs
        out_specs.append(tok)
        out_shape.append(jax.ShapeDtypeStruct((n, D_MODEL), F32))
        scratch = [pltpu.VMEM((GATHER_SLOTS, TOKEN_TILE, D_MODEL), F32),
                   pltpu.SemaphoreType.DMA((GATHER_SLOTS,))]
    outs = pl.pallas_call(
        functools.partial(_qkv_kernel, q_scale=q_scale, v_transposed=wvt_bf16 is not None,
                          pending=pending is not None),
        grid_spec=pltpu.PrefetchScalarGridSpec(
            num_scalar_prefetch=len(prefetch), grid=(n // TOKEN_TILE,),
            in_specs=in_specs, out_specs=out_specs, scratch_shapes=scratch),
        out_shape=out_shape,
        compiler_params=_params(1),
        name="qkv_proj",
    )(*prefetch, x, mods, g, w_bf16, wv, qg, kg, bd, *extra)
    return (*outs[:3], outs[3] if pending is not None else x)


def _pair_attention(q2, score_fn, value_fn):
    lane = lax.broadcasted_iota(jnp.int32, q2.shape, 1)
    first = lane < HEAD_DIM
    outs = []
    for keep in (first, jnp.logical_not(first)):
        s = score_fn(jnp.where(keep, q2, jnp.zeros_like(q2)))
        m = functools.reduce(jnp.maximum, [jnp.max(b, axis=-1, keepdims=True) for b in s])
        p = [jnp.exp(b - m) for b in s]
        l = functools.reduce(jnp.add, [jnp.sum(b, axis=-1, keepdims=True) for b in p])
        outs.append(value_fn([b.astype(BF16) for b in p]) / l)
    return jnp.where(first, outs[0], outs[1])


def _ctx_attn_kernel(x_ref, mod_ref, q_ref, k_ref, v_ref, wo_ref, o_ref, att_ref):
    for hp in range(N_HEADS // 2):
        cols = slice(hp * LANES, (hp + 1) * LANES)
        k2 = k_ref[:, cols].astype(BF16)
        v2 = v_ref[:, cols].astype(BF16)
        att = _pair_attention(
            q_ref[:, cols],
            lambda q: [_dot_nt(q, k2)],
            lambda p: jnp.dot(p[0], v2, preferred_element_type=F32))
        att_ref[:, cols] = att.astype(BF16)
    y = jnp.dot(att_ref[...], wo_ref[...], preferred_element_type=F32)
    o_ref[...] = x_ref[...] + mod_ref[...][2:3] * y


def _ctx_attention(x, mods, layer, q, k, v, wo_bf16, seq):
    n = x.shape[0]
    row = lambda i: (i, 0)
    tok = pl.BlockSpec((seq, D_MODEL), row)
    return pl.pallas_call(
        _ctx_attn_kernel,
        grid=(n // seq,),
        in_specs=[tok, _mod_spec(layer, None), tok, tok, tok,
                  pl.BlockSpec((D_MODEL, D_MODEL), lambda i: (0, 0))],
        out_specs=tok,
        out_shape=jax.ShapeDtypeStruct((n, D_MODEL), F32),
        scratch_shapes=[pltpu.VMEM((seq, D_MODEL), BF16)],
        compiler_params=_params(1),
        name="context_attention",
    )(x, mods, q, k, v, wo_bf16)


def _na_kernel(x_ref, mod_ref, q_ref, kp_ref, kc_ref, kn_ref, vp_ref, vc_ref, vn_ref,
               kctx_ref, vctx_ref, bias_ref, wo_ref, o_ref, att_ref, *, n_grid_rows):
    rb = pl.program_id(1)
    entry = []
    for s in range(NA_SLOTS):
        kr = (rb - 1) * NA_ROWS + s
        per_row = []
        for i in range(NA_ROWS):
            r = rb * NA_ROWS + i
            rs = jnp.clip(r - WIN_ROWS // 2, 0, n_grid_rows - WIN_ROWS)
            in_band = jnp.logical_and(kr >= rs, kr < rs + WIN_ROWS)
            per_row.append(jnp.where(in_band, s - i + (WIN_ROWS - 1 - NA_ROWS), BIAS_NONE))
        entry.append(per_row)

    lane = lax.broadcasted_iota(jnp.int32, (GRID_W, LANES), 1)
    left = lane < GRID_W
    qlane = lax.broadcasted_iota(jnp.int32, (NA_TOKENS, LANES), 1)
    k_blocks = (kp_ref, kc_ref, kn_ref)
    v_blocks = (vp_ref, vc_ref, vn_ref)
    for hp in range(N_HEADS // 2):
        cols = slice(hp * LANES, (hp + 1) * LANES)
        q2 = q_ref[:, cols]
        halves = []
        for sub in range(2):
            head = 2 * hp + sub
            keep = (qlane < HEAD_DIM) if sub == 0 else (qlane >= HEAD_DIM)
            qm = jnp.where(keep, q2, jnp.zeros_like(q2))
            s_blocks = []
            for j in range(3):
                st = _dot_nt(k_blocks[j][:, cols], qm)
                rows = []
                for sr in range(NA_ROWS):
                    s = j * NA_ROWS + sr
                    bias = jnp.concatenate(
                        [jnp.where(left, bias_ref[head, entry[s][2 * t]],
                                   bias_ref[head, entry[s][2 * t + 1]])
                         for t in range(NA_ROWS // 2)], axis=-1)
                    rows.append(st[sr * GRID_W:(sr + 1) * GRID_W, :] + bias)
                s_blocks.append(jnp.concatenate(rows, axis=0))
            s_blocks.append(_dot_nt(kctx_ref[:, cols], qm))
            m = functools.reduce(jnp.maximum,
                                 [jnp.max(b, axis=0, keepdims=True) for b in s_blocks])
            p = [jnp.exp2(b - m).astype(BF16) for b in s_blocks]
            vrows = slice(hp * LANES + sub * HEAD_DIM, hp * LANES + (sub + 1) * HEAD_DIM)
            vt = [jnp.concatenate([vb[vrows, :], jnp.ones((ONES_ROWS, vb.shape[1]), BF16)],
                                  axis=0) for vb in v_blocks + (vctx_ref,)]
            ot = functools.reduce(jnp.add, [
                jnp.dot(vt[j], p[j], preferred_element_type=F32) for j in range(4)])
            halves.append(ot[:HEAD_DIM] / ot[HEAD_DIM:HEAD_DIM + 1])
        att_ref[:, cols] = jnp.concatenate(halves, axis=0).T.astype(BF16)
    y = jnp.dot(att_ref[...], wo_ref[...], preferred_element_type=F32)
    o_ref[...] = x_ref[...] + mod_ref[...][2:3] * y


def _neighbourhood_attention(x, mods, layer, q, k, vt, kctx, vctx_t, bias, wo_bf16, batch, seq):
    n_grid_rows = seq // GRID_W
    nrb = n_grid_rows // NA_ROWS
    cur = lambda b, r: (b * nrb + r, 0)
    prev = lambda b, r: (b * nrb + jnp.maximum(r - 1, 0), 0)
    nxt = lambda b, r: (b * nrb + jnp.minimum(r + 1, nrb - 1), 0)
    blk = lambda f: pl.BlockSpec((NA_TOKENS, D_MODEL), f)
    blk_t = lambda f: pl.BlockSpec((None, D_MODEL, NA_TOKENS),
                                   lambda b, r: (f(b, r)[0], 0, 0))
    n_ctx = kctx.shape[1]
    return pl.pallas_call(
        functools.partial(_na_kernel, n_grid_rows=n_grid_rows),
        grid=(batch, nrb),
        in_specs=[blk(cur),
                  pl.BlockSpec((None, None, 6, D_MODEL), lambda b, r: (layer, 1 + b, 0, 0)),
                  blk(cur), blk(prev), blk(cur), blk(nxt),
                  blk_t(prev), blk_t(cur), blk_t(nxt),
                  pl.BlockSpec((None, n_ctx, D_MODEL), lambda b, r: (b, 0, 0)),
                  pl.BlockSpec((None, D_MODEL, n_ctx), lambda b, r: (b, 0, 0)),
                  pl.BlockSpec(bias.shape, lambda b, r: (0, 0, 0, 0)),
                  pl.BlockSpec((D_MODEL, D_MODEL), lambda b, r: (0, 0))],
        out_specs=blk(cur),
        out_shape=jax.ShapeDtypeStruct(x.shape, F32),
        scratch_shapes=[pltpu.VMEM((NA_TOKENS, D_MODEL), BF16)],
        compiler_params=_params(2),
        name="neighbourhood_attention",
    )(x, mods, q, k, k, k, vt, vt, vt, kctx, vctx_t, bias, wo_bf16)


def _na_bias_table(rpb_layer):
    qc = np.arange(GRID_W)[None, :]
    kc = np.arange(GRID_W)[:, None]
    start = np.clip(qc - WIN_COLS // 2, 0, GRID_W - WIN_COLS)
    valid = (kc >= start) & (kc < start + WIN_COLS)
    off = kc - qc + WIN_COLS - 1
    select = ((off[None] == np.arange(2 * WIN_COLS - 1)[:, None, None]) & valid[None])
    m = jnp.einsum('hro,okq->hrkq', rpb_layer * LOG2_E, jnp.asarray(select, F32),
                   precision=lax.Precision.HIGHEST)
    m = m + jnp.asarray(np.where(valid, 0.0, NEG), F32)
    m = jnp.concatenate([m, jnp.full_like(m[:, :1], NEG)], axis=1)
    return jnp.concatenate([m, m], axis=-1)


def _pool_kernel(*refs, tile, seq, n_tokens, pending):
    if pending:
        (dest_ref, x_ref, xp_ref, xn_ref, mod_ref, g_ref, w_ref, ps_ref, pmod_ref, y_hbm,
         o_ref, h_ref, ybuf, sems) = refs

        def index_of(step, r):
            pos = step * tile + (r - POOL_HALO)
            inside = POOL_HALO <= r < POOL_HALO + tile
            return pos if inside else jnp.clip(pos, 0, n_tokens - 1)

        y, drain = _gathered_rows(dest_ref, y_hbm, ybuf, sems, index_of)
        gate = pmod_ref[...][5:6]
        x_prev = xp_ref[...] + gate * y[:POOL_HALO]
        x_cur = x_ref[...] + gate * y[POOL_HALO:POOL_HALO + tile]
        x_next = xn_ref[...] + gate * y[POOL_HALO + tile:]
    else:
        (x_ref, xp_ref, xn_ref, mod_ref, g_ref, w_ref, ps_ref, o_ref, h_ref) = refs
        x_prev, x_cur, x_next = xp_ref[...], x_ref[...], xn_ref[...]
        drain = lambda: None
    i = pl.program_id(0)
    tiles_per_seq = seq // tile
    t_in_seq = i % tiles_per_seq
    m = mod_ref[...]
    g = g_ref[...]
    h_cur = _rms_modulate(x_cur, g, m[0:1], m[1:2])
    h_prev = _rms_modulate(x_prev, g, m[0:1], m[1:2])
    h_next = _rms_modulate(x_next, g, m[0:1], m[1:2])
    h_ref[0:POOL_HALO, :] = jnp.where(t_in_seq > 0, h_prev, 0.0)
    h_ref[POOL_HALO:POOL_HALO + tile, :] = h_cur
    h_ref[POOL_HALO + tile:, :] = jnp.where(t_in_seq < tiles_per_seq - 1, h_next, 0.0)

    pos = t_in_seq * tile + lax.broadcasted_iota(jnp.int32, (tile, 1), 0)
    ys = []
    for grp, w in enumerate(POOL_SIZES):
        cols = slice(grp * POOL_GROUP_DIM, (grp + 1) * POOL_GROUP_DIM)
        total = h_ref[POOL_HALO - w // 2:POOL_HALO - w // 2 + tile, cols]
        for d in range(1 - w // 2, w - w // 2):
            total = total + h_ref[POOL_HALO + d:POOL_HALO + d + tile, cols]
        lo = jnp.clip(pos - w // 2, 0, seq)
        hi = jnp.clip(pos - w // 2 + w, 0, seq)
        pooled = total / (hi - lo).astype(F32)
        diff = (pooled - h_cur[:, cols]).astype(BF16)
        ys.append(jnp.dot(diff, w_ref[grp], preferred_element_type=F32))
    y_mix = jnp.concatenate(ys, axis=-1) * ps_ref[...]
    o_ref[...] = x_cur + m[2:3] * y_mix
    drain()


def _pool_mixer(x, mods, layer, tiles_per_cond_row, g, w_pool_bf16, pool_scale, tile, seq,
                pending):
    n = x.shape[0]
    hb = tile // POOL_HALO
    last = n // POOL_HALO - 1
    const = lambda i, *_: (0, 0)
    in_specs = [pl.BlockSpec((tile, D_MODEL), lambda i, *_: (i, 0)),
                pl.BlockSpec((POOL_HALO, D_MODEL),
                             lambda i, *_: (jnp.maximum(i * hb - 1, 0), 0)),
                pl.BlockSpec((POOL_HALO, D_MODEL),
                             lambda i, *_: (jnp.minimum((i + 1) * hb, last), 0)),
                _mod_spec(layer, tiles_per_cond_row),
                pl.BlockSpec((1, D_MODEL), const),
                pl.BlockSpec(w_pool_bf16.shape, lambda i, *_: (0, 0, 0)),
                pl.BlockSpec((1, D_MODEL), const)]
    scratch = [pltpu.VMEM((tile + 2 * POOL_HALO, D_MODEL), F32)]
    prefetch, extra = (), ()
    if pending is not None:
        prefetch, extra, extra_specs = _pending_args(pending, mods, tiles_per_cond_row)
        in_specs += extra_specs
        scratch += [pltpu.VMEM((GATHER_SLOTS, tile + 2 * POOL_HALO, D_MODEL), F32),
                    pltpu.SemaphoreType.DMA((GATHER_SLOTS,))]
    return pl.pallas_call(
        functools.partial(_pool_kernel, tile=tile, seq=seq, n_tokens=n,
                          pending=pending is not None),
        grid_spec=pltpu.PrefetchScalarGridSpec(
            num_scalar_prefetch=len(prefetch), grid=(n // tile,),
            in_specs=in_specs,
            out_specs=pl.BlockSpec((tile, D_MODEL), lambda i, *_: (i, 0)),
            scratch_shapes=scratch),
        out_shape=jax.ShapeDtypeStruct(x.shape, F32),
        compiler_params=_params(1),
        name="pool_mixer",
    )(*prefetch, x, x, x, mods, g, w_pool_bf16, pool_scale, *extra)


def _first_argmax(vals):
    best = functools.reduce(jnp.maximum, vals)
    idx = jnp.full(best.shape, len(vals) - 1, jnp.int32)
    for j in range(len(vals) - 2, -1, -1):
        idx = jnp.where(vals[j] == best, j, idx)
    return best, idx


def _router_kernel(x_ref, mod_ref, g_ref, wr_hi_ref, wr_lo_ref, br_ref, aux_ref, rows_ref):
    m = mod_ref[...]
    h = _rms_modulate(x_ref[...], g_ref[...], m[3:4], m[4:5])
    h_hi = h.astype(BF16)
    h_lo = (h - h_hi.astype(F32)).astype(BF16)
    w_hi = wr_hi_ref[...]
    both = _dot_nt(jnp.concatenate([w_hi, wr_lo_ref[...]], axis=0), h_hi)
    logits = (both[:ROUTER_ROWS] + _dot_nt(w_hi, h_lo) + both[ROUTER_ROWS:]) + br_ref[...]
    row = lambda r: logits[r:r + 1, :]
    grp = [row(j) for j in range(N_EXPERT_GROUPS)]
    g_max, g_sel = _first_argmax(grp)
    g_w = 1.0 / functools.reduce(jnp.add, [jnp.exp(v - g_max) for v in grp])
    e_in = []
    for j in range(EXPERTS_PER_GROUP):
        v = row(N_EXPERT_GROUPS + (N_EXPERT_GROUPS - 1) * EXPERTS_PER_GROUP + j)
        for gi in range(N_EXPERT_GROUPS - 2, -1, -1):
            v = jnp.where(g_sel == gi, row(N_EXPERT_GROUPS + gi * EXPERTS_PER_GROUP + j), v)
        e_in.append(v)
    v1, i1 = _first_argmax(e_in)
    v2, i2 = _first_argmax([jnp.where(i1 == j, -jnp.inf, e_in[j])
                            for j in range(EXPERTS_PER_GROUP)])
    t = jnp.exp(v2 - v1)
    w1 = g_w / (1.0 + t)
    w2 = g_w * t / (1.0 + t)
    lo = jnp.minimum(i1, i2)
    hi = jnp.maximum(i1, i2)
    w_lo = jnp.where(i1 < i2, w1, w2)
    w_hi = jnp.where(i1 < i2, w2, w1)
    pair = jnp.where(lo == 0, hi - 1, jnp.where(lo == 1, hi + 1, 5))
    bucket = (g_sel * N_PAIRS + pair).astype(F32)
    n_tok = bucket.shape[1]
    aux_ref[...] = jnp.concatenate([bucket, jnp.zeros((7, n_tok), F32)], axis=0)
    rows_ref[:, :D_MODEL] = h
    rows_ref[:, D_MODEL:] = jnp.concatenate(
        [w_lo, w_hi, jnp.zeros((LANES - 2, n_tok), F32)], axis=0).T


def _router(x, mods, layer, tiles_per_cond_row, g, wr_hi, wr_lo, br):
    n = x.shape[0]
    const = lambda i: (0, 0)
    tok = pl.BlockSpec((TOKEN_TILE, D_MODEL), lambda i: (i, 0))
    return pl.pallas_call(
        _router_kernel,
        grid=(n // TOKEN_TILE,),
        in_specs=[tok, _mod_spec(layer, tiles_per_cond_row),
                  pl.BlockSpec((1, D_MODEL), const),
                  pl.BlockSpec((ROUTER_ROWS, D_MODEL), const),
                  pl.BlockSpec((ROUTER_ROWS, D_MODEL), const),
                  pl.BlockSpec((ROUTER_ROWS, 1), const)],
        out_specs=[pl.BlockSpec((8, TOKEN_TILE), lambda i: (0, i)),
                   pl.BlockSpec((TOKEN_TILE, ROW_W), lambda i: (i, 0))],
        out_shape=[jax.ShapeDtypeStruct((8, n), F32),
                   jax.ShapeDtypeStruct((n, ROW_W), F32)],
        compiler_params=_params(1),
        name="moe_router",
    )(x, mods, g, wr_hi, wr_lo, br)


def _expert_kernel(dest_ref, pad_at_ref, elo_ref, ehi_ref, nused_ref, rows_hbm,
                   wg_lo, wu_lo, wd_lo, wg_hi, wu_hi, wd_hi, o_ref, perm, xbuf, sems,
                   *, n_tokens):
    t = pl.program_id(0)
    n_used = nused_ref[0]
    slot = t % GATHER_SLOTS

    def fetch(tile, s):
        for r in range(MOE_TILE):
            pltpu.make_async_copy(rows_hbm.at[pl.ds(perm[tile * MOE_TILE + r], 1)],
                                  xbuf.at[s].at[pl.ds(r, 1)], sems.at[s]).start()

    @pl.when(t == 0)
    def _():
        for b in range(N_BUCKETS):
            @pl.when(pad_at_ref[b] >= 0)
            def _():
                def fill(r, c):
                    perm[pad_at_ref[b] + r] = 0
                    return c
                lax.fori_loop(0, MOE_TILE, fill, 0, unroll=16)

        def place(n, c):
            perm[dest_ref[n]] = n
            return c
        lax.fori_loop(0, n_tokens, place, 0, unroll=16)
        fetch(0, 0)
        fetch(jnp.minimum(1, n_used - 1), 1)

    @pl.when(t < n_used)
    def _():
        _wait_rows(rows_hbm, xbuf.at[slot], sems.at[slot])
        ahead = (t + 2) % GATHER_SLOTS
        fetch(jnp.minimum(t + 2, n_used - 1), ahead)
        x = xbuf[slot, :, :D_MODEL].astype(BF16)
        cw = xbuf[slot, :, D_MODEL:]
        y = None
        for e, (wg, wu, wd) in enumerate(((wg_lo, wu_lo, wd_lo), (wg_hi, wu_hi, wd_hi))):
            a = jnp.dot(x, wg[...], preferred_element_type=F32)
            u = jnp.dot(x, wu[...], preferred_element_type=F32)
            act = (a / (1.0 + jnp.exp(-a))) * u * cw[:, e:e + 1]
            part = jnp.dot(act.astype(BF16), wd[...], preferred_element_type=F32)
            y = part if y is None else y + part
        o_ref[...] = y

        @pl.when(t == n_used - 1)
        def _():
            for s in (ahead, (t + 1) % GATHER_SLOTS):
                _wait_rows(rows_hbm, xbuf.at[s], sems.at[s])

    @pl.when(t >= n_used)
    def _():
        o_ref[...] = jnp.zeros_like(o_ref)


def _experts(rows, dest, pad_at, e_lo, e_hi, n_used, n_pad, layer, wg, wu, wd):
    w_lo = lambda t, dest, pad, elo, ehi, nu: (layer, elo[t], 0, 0)
    w_hi = lambda t, dest, pad, elo, ehi, nu: (layer, ehi[t], 0, 0)
    up = lambda f: pl.BlockSpec((None, None, D_MODEL, D_FF), f)
    down = lambda f: pl.BlockSpec((None, None, D_FF, D_MODEL), f)
    return pl.pallas_call(
        functools.partial(_expert_kernel, n_tokens=rows.shape[0]),
        grid_spec=pltpu.PrefetchScalarGridSpec(
            num_scalar_prefetch=5,
            grid=(n_pad // MOE_TILE,),
            in_specs=[pl.BlockSpec(memory_space=pl.ANY),
                      up(w_lo), up(w_lo), down(w_lo), up(w_hi), up(w_hi), down(w_hi)],
            out_specs=pl.BlockSpec((MOE_TILE, D_MODEL), lambda t, *_: (t, 0)),
            scratch_shapes=[pltpu.SMEM((n_pad,), jnp.int32),
                            pltpu.VMEM((GATHER_SLOTS, MOE_TILE, ROW_W), F32),
                            pltpu.SemaphoreType.DMA((GATHER_SLOTS,))]),
        out_shape=jax.ShapeDtypeStruct((n_pad, D_MODEL), F32),
        compiler_params=_params(1),
        name="moe_experts",
    )(dest, pad_at, e_lo, e_hi, n_used, rows, wg, wu, wd, wg, wu, wd)


def _unsort_kernel(dest_ref, x_ref, mod_ref, y_hbm, o_ref, ybuf, sems):
    y, drain = _gathered_rows(dest_ref, y_hbm, ybuf, sems,
                              lambda step, r: step * TOKEN_TILE + r)
    o_ref[...] = x_ref[...] + mod_ref[...][5:6] * y
    drain()


def _unsort_residual(x, mods, tiles_per_cond_row, pending):
    n = x.shape[0]
    tok = pl.BlockSpec((TOKEN_TILE, D_MODEL), lambda i, *_: (i, 0))
    prefetch, extra, extra_specs = _pending_args(pending, mods, tiles_per_cond_row)
    return pl.pallas_call(
        _unsort_kernel,
        grid_spec=pltpu.PrefetchScalarGridSpec(
            num_scalar_prefetch=len(prefetch),
            grid=(n // TOKEN_TILE,),
            in_specs=[tok] + extra_specs,
            out_specs=tok,
            scratch_shapes=[pltpu.VMEM((GATHER_SLOTS, TOKEN_TILE, D_MODEL), F32),
                            pltpu.SemaphoreType.DMA((GATHER_SLOTS,))]),
        out_shape=jax.ShapeDtypeStruct(x.shape, F32),
        compiler_params=_params(1),
        name="moe_unsort_residual",
    )(*prefetch, x, *extra)


_PAIR_LO = (0, 0, 0, 1, 1, 2)
_PAIR_HI = (1, 2, 3, 2, 3, 3)


def _dispatch_plan(aux, n):
    n_tiles = n // MOE_TILE + N_BUCKETS
    bucket = aux[0].astype(jnp.int32)
    onehot = (bucket[:, None] == jnp.arange(N_BUCKETS)[None, :]).astype(jnp.int32)
    csum = jnp.cumsum(onehot, axis=0)
    counts = csum[-1]
    tiles = (counts + MOE_TILE - 1) // MOE_TILE
    tile_end = jnp.cumsum(tiles)
    tile_start = tile_end - tiles
    n_used = tile_end[-1]
    slot0 = tile_start * MOE_TILE - 1
    dest = jnp.sum(onehot * (csum + slot0[None, :]), axis=1).astype(jnp.int32)
    pad_at = jnp.where(tiles > 0, (tile_end - 1) * MOE_TILE, -1).astype(jnp.int32)
    t = jnp.minimum(jnp.arange(n_tiles), n_used - 1)
    tile_bucket = jnp.sum((t[:, None] >= tile_end[None, :]).astype(jnp.int32), axis=1)
    grp = tile_bucket // N_PAIRS
    pair = tile_bucket % N_PAIRS
    e_lo = grp * EXPERTS_PER_GROUP + jnp.asarray(_PAIR_LO, jnp.int32)[pair]
    e_hi = grp * EXPERTS_PER_GROUP + jnp.asarray(_PAIR_HI, jnp.int32)[pair]
    return (dest, pad_at, e_lo.astype(jnp.int32), e_hi.astype(jnp.int32),
            n_used.astype(jnp.int32).reshape(1), n_tiles * MOE_TILE)


def _moe(x, mods, layer, tiles_per_cond_row, g, router_w, expert_w):
    aux, rows = _router(x, mods, layer, tiles_per_cond_row, g, *router_w)
    dest, pad_at, e_lo, e_hi, n_used, n_pad = _dispatch_plan(aux, x.shape[0])
    y_sorted = _experts(rows, dest, pad_at, e_lo, e_hi, n_used, n_pad, layer, *expert_w)
    return dest, y_sorted, layer


def kernel(x_prompt, x_sample, cache_k, cache_v, c, c_ctx, norm1_g, norm2_g, w_mod, b_mod,
           w_qkv, q_norm_g, k_norm_g, rpb, w_o, w_pool, pool_scale, w_router_group,
           b_router_group, w_router_expert, b_router_expert, w_gate, w_up, w_down):
    bp, lp, _ = x_prompt.shape
    bs, ls, _ = x_sample.shape
    depth = w_mod.shape[0]
    assert bs + 1 <= COND_ROWS and ls % TOKEN_TILE == 0 and (bp * lp) % TOKEN_TILE == 0
    assert lp <= TOKEN_TILE and TOKEN_TILE % lp == 0

    cond = jnp.zeros((COND_ROWS, D_MODEL), F32).at[0].set(c_ctx).at[1:1 + bs].set(c)
    mods = _modulation(cond, w_mod, b_mod)

    xp = x_prompt.reshape(bp * lp, D_MODEL)
    xs = x_sample.reshape(bs * ls, D_MODEL)
    s_tiles = ls // TOKEN_TILE

    head_of = jnp.arange(MXU_DIM) // HEAD_DIM
    ones_bd = (head_of[:, None] == head_of[None, :]).astype(BF16)

    expert_w = (w_gate.astype(BF16), w_up.astype(BF16), w_down.astype(BF16))
    new_k, new_v = [], []
    pend_p = pend_s = None
    for l in range(depth):
        j = l // 2
        g1 = norm1_g[l][None]
        if l % 2 == 0:
            w_bf16 = w_qkv[j].astype(BF16)
            wvt = w_bf16[:, 2 * D_MODEL:].T
            wo_bf16 = w_o[j].astype(BF16)
            qg = jnp.tile(q_norm_g[j], N_HEADS)[None]
            kg = jnp.tile(k_norm_g[j], N_HEADS)[None]
            scale = HEAD_DIM ** -0.5
            qp, kp, vp, xp = _qkv(xp, mods, l, None, g1, w_bf16, None, qg, kg, ones_bd, F32,
                                  scale, pend_p)
            new_k.append(kp.reshape(bp, lp, N_HEADS, HEAD_DIM))
            new_v.append(vp.reshape(bp, lp, N_HEADS, HEAD_DIM))
            xp = _ctx_attention(xp, mods, l, qp, kp, vp, wo_bf16, lp)
            qs, ks, vts, xs = _qkv(xs, mods, l, s_tiles, g1, w_bf16, wvt, qg, kg, ones_bd,
                                   BF16, scale * LOG2_E, pend_s)
            kctx = cache_k[:, j].reshape(bs, -1, D_MODEL).astype(BF16)
            vctx_t = jnp.swapaxes(cache_v[:, j].reshape(bs, -1, D_MODEL), 1, 2).astype(BF16)
            xs = _neighbourhood_attention(xs, mods, l, qs, ks, vts, kctx, vctx_t,
                                          _na_bias_table(rpb[j]), wo_bf16, bs, ls)
        else:
            wp = w_pool[j].astype(BF16)
            ps = pool_scale[j][None]
            xp = _pool_mixer(xp, mods, l, None, g1, wp, ps, lp, lp, pend_p)
            xs = _pool_mixer(xs, mods, l, s_tiles, g1, wp, ps, TOKEN_TILE, ls, pend_s)

        wr = jnp.zeros((ROUTER_ROWS, D_MODEL), F32)
        wr = wr.at[:N_EXPERT_GROUPS].set(w_router_group[l].T)
        wr = wr.at[N_EXPERT_GROUPS:N_EXPERT_GROUPS + N_EXPERTS].set(w_router_expert[l].T)
        wr_hi = wr.astype(BF16)
        wr_lo = (wr - wr_hi.astype(F32)).astype(BF16)
        br = jnp.zeros((ROUTER_ROWS, 1), F32)
        br = br.at[:N_EXPERT_GROUPS, 0].set(b_router_group[l])
        br = br.at[N_EXPERT_GROUPS:N_EXPERT_GROUPS + N_EXPERTS, 0].set(b_router_expert[l])
        router_w = (wr_hi, wr_lo, br)
        g2 = norm2_g[l][None]
        pend_p = _moe(xp, mods, l, None, g2, router_w, expert_w)
        pend_s = _moe(xs, mods, l, s_tiles, g2, router_w, expert_w)

    xp = _unsort_residual(xp, mods, None, pend_p)
    xs = _unsort_residual(xs, mods, s_tiles, pend_s)
    return (xp.reshape(bp, lp, D_MODEL), xs.reshape(bs, ls, D_MODEL),
            jnp.stack(new_k, axis=1), jnp.stack(new_v, axis=1))
```

```python
import functools

import jax
import jax.numpy as jnp
import numpy as np
from jax import lax
from jax.experimental import pallas as pl
from jax.experimental.pallas import tpu as pltpu

D_MODEL = 1024
N_HEADS = 16
HEAD_DIM = D_MODEL // N_HEADS
GRID_W = 64
WIN_ROWS = 8
WIN_COLS = 16
POOL_SIZES = (2, 4, 8, 16)
POOL_GROUP_DIM = D_MODEL // len(POOL_SIZES)
POOL_HALO = 8
N_EXPERT_GROUPS = 4
EXPERTS_PER_GROUP = 4
N_EXPERTS = N_EXPERT_GROUPS * EXPERTS_PER_GROUP
N_PAIRS = 6
N_BUCKETS = N_EXPERT_GROUPS * N_PAIRS
D_FF = D_MODEL // 2
EPS = 1e-6
NEG = -1e30

LANES = 128
MXU_DIM = 256
TOKEN_TILE = 512
MOE_TILE = 256
ROUTER_ROWS = 32
COND_ROWS = 16
VMEM_LIMIT = 56 * 1024 * 1024

NA_ROWS = 4
NA_TOKENS = NA_ROWS * GRID_W
NA_SLOTS = 3 * NA_ROWS
BIAS_NONE = 2 * WIN_ROWS - 1
ONES_ROWS = 16
LOG2_E = 1.4426950408889634
ROW_W = D_MODEL + LANES
GATHER_SLOTS = 3

BF16 = jnp.bfloat16
F32 = jnp.float32


def _params(n_grid_dims, vmem=VMEM_LIMIT):
    return pltpu.CompilerParams(
        dimension_semantics=("arbitrary",) * n_grid_dims, vmem_limit_bytes=vmem)


def _rms_modulate(x, g, shift, scale):
    y = x * lax.rsqrt(jnp.mean(x * x, axis=-1, keepdims=True) + EPS)
    return (y * g) * (1.0 + scale) + shift


def _dot_nt(a, b):
    return lax.dot_general(a, b, (((1,), (1,)), ((), ())), preferred_element_type=F32)


def _mod_kernel(cond_ref, w_ref, b_ref, o_ref):
    c = cond_ref[...]
    s = c / (1.0 + jnp.exp(-c))
    o_ref[...] = jnp.dot(s, w_ref[...], preferred_element_type=F32,
                         precision=lax.Precision.HIGHEST) + b_ref[...]


def _modulation(cond, w_mod, b_mod):
    depth = w_mod.shape[0]
    tn = 1536
    out = pl.pallas_call(
        _mod_kernel,
        grid=(depth, 6 * D_MODEL // tn),
        in_specs=[
            pl.BlockSpec((COND_ROWS, D_MODEL), lambda l, n: (0, 0)),
            pl.BlockSpec((None, D_MODEL, tn), lambda l, n: (l, 0, n)),
            pl.BlockSpec((None, 1, tn), lambda l, n: (l, 0, n)),
        ],
        out_specs=pl.BlockSpec((None, COND_ROWS, tn), lambda l, n: (l, 0, n)),
        out_shape=jax.ShapeDtypeStruct((depth, COND_ROWS, 6 * D_MODEL), F32),
        compiler_params=_params(2),
        name="adaln_modulation",
    )(cond, w_mod, b_mod.reshape(depth, 1, 6 * D_MODEL))
    return out.reshape(depth, COND_ROWS, 6, D_MODEL)


def _mod_spec(layer, tiles_per_cond_row):
    if tiles_per_cond_row is None:
        return pl.BlockSpec((None, None, 6, D_MODEL), lambda i, *_: (layer, 0, 0, 0))
    return pl.BlockSpec((None, None, 6, D_MODEL),
                        lambda i, *_: (layer, 1 + i // tiles_per_cond_row, 0, 0))


def _wait_rows(hbm, buf, sem):
    pltpu.make_async_copy(hbm.at[pl.ds(0, buf.shape[0])], buf, sem).wait()


def _gathered_rows(idx_ref, y_hbm, ybuf, sems, index_of):
    i = pl.program_id(0)
    last = pl.num_programs(0) - 1
    slot = i % GATHER_SLOTS

    def fetch(step, s):
        for r in range(ybuf.shape[1]):
            pltpu.make_async_copy(y_hbm.at[pl.ds(idx_ref[index_of(step, r)], 1)],
                                  ybuf.at[s].at[pl.ds(r, 1)], sems.at[s]).start()

    @pl.when(i == 0)
    def _():
        fetch(0, 0)
        fetch(jnp.minimum(1, last), 1)

    _wait_rows(y_hbm, ybuf.at[slot], sems.at[slot])
    y = ybuf[slot]
    ahead = (i + 2) % GATHER_SLOTS
    fetch(jnp.minimum(i + 2, last), ahead)

    def drain():
        @pl.when(i == last)
        def _():
            for s in (ahead, (i + 1) % GATHER_SLOTS):
                _wait_rows(y_hbm, ybuf.at[s], sems.at[s])

    return y, drain


def _pending_args(pending, mods, tiles_per_cond_row):
    dest, y_sorted, moe_layer = pending
    return ((dest,), (mods, y_sorted),
            [_mod_spec(moe_layer, tiles_per_cond_row), pl.BlockSpec(memory_space=pl.ANY)])


def _head_rms(t, ones_bd, gain):
    sq = (t * t).astype(BF16)
    parts = [jnp.dot(sq[:, c:c + MXU_DIM], ones_bd, preferred_element_type=F32)
             for c in range(0, D_MODEL, MXU_DIM)]
    mean = jnp.concatenate(parts, axis=-1) / HEAD_DIM
    return t * lax.rsqrt(mean + EPS) * gain


def _qkv_kernel(*refs, q_scale, v_transposed, pending):
    if pending:
        (dest_ref, x_ref, mod_ref, g_ref, wqk_ref, wv_ref, qg_ref, kg_ref, bd_ref,
         pmod_ref, y_hbm, q_ref, k_ref, v_ref, xnew_ref, ybuf, sems) = refs
        y, drain = _gathered_rows(dest_ref, y_hbm, ybuf, sems,
                                  lambda step, r: step * TOKEN_TILE + r)
        x = x_ref[...] + pmod_ref[...][5:6] * y
        xnew_ref[...] = x
    else:
        (x_ref, mod_ref, g_ref, wqk_ref, wv_ref, qg_ref, kg_ref, bd_ref,
         q_ref, k_ref, v_ref) = refs
        x, drain = x_ref[...], lambda: None
    m = mod_ref[...]
    h = _rms_modulate(x, g_ref[...], m[0:1], m[1:2]).astype(BF16)
    qk = jnp.dot(h, wqk_ref[...], preferred_element_type=F32)
    bd = bd_ref[...]
    q = _head_rms(qk[:, :D_MODEL], bd, qg_ref[...])
    k = _head_rms(qk[:, D_MODEL:], bd, kg_ref[...])
    q_ref[...] = (q * q_scale).astype(q_ref.dtype)
    k_ref[...] = k.astype(k_ref.dtype)
    if v_transposed:
        vt = _dot_nt(wv_ref[...], h).astype(v_ref.dtype)
        for j in range(TOKEN_TILE // NA_TOKENS):
            v_ref[j] = vt[:, j * NA_TOKENS:(j + 1) * NA_TOKENS]
    else:
        v_ref[...] = jnp.dot(h, wv_ref[...], preferred_element_type=F32).astype(v_ref.dtype)
    drain()


def _qkv(x, mods, layer, tiles_per_cond_row, g, w_bf16, wvt_bf16, qg, kg, bd, kv_dtype,
         q_scale, pending):
    n = x.shape[0]
    row = lambda i, *_: (i, 0)
    const = lambda i, *_: (0, 0)
    tok = pl.BlockSpec((TOKEN_TILE, D_MODEL), row)
    vec = pl.BlockSpec((1, D_MODEL), const)
    if wvt_bf16 is not None:
        per_tile = TOKEN_TILE // NA_TOKENS
        v_spec = pl.BlockSpec((per_tile, D_MODEL, NA_TOKENS), lambda i, *_: (i, 0, 0))
        v_shape = jax.ShapeDtypeStruct((n // NA_TOKENS, D_MODEL, NA_TOKENS), kv_dtype)
        wv, wv_spec = wvt_bf16, pl.BlockSpec((D_MODEL, D_MODEL), const)
    else:
        v_spec, v_shape = tok, jax.ShapeDtypeStruct((n, D_MODEL), kv_dtype)
        wv, wv_spec = w_bf16, pl.BlockSpec((D_MODEL, D_MODEL), lambda i, *_: (0, 2))
    in_specs = [tok, _mod_spec(layer, tiles_per_cond_row), vec,
                pl.BlockSpec((D_MODEL, 2 * D_MODEL), const), wv_spec, vec, vec,
                pl.BlockSpec((MXU_DIM, MXU_DIM), const)]
    out_specs = [tok, tok, v_spec]
    out_shape = [jax.ShapeDtypeStruct((n, D_MODEL), BF16),
                 jax.ShapeDtypeStruct((n, D_MODEL), kv_dtype), v_shape]
    prefetch, extra, scratch = (), (), []
    if pending is not None:
        prefetch, extra, extra_specs = _pending_args(pending, mods, tiles_per_cond_row)
        in_specs += extra_specs
        out_specs.append(tok)
        out_shape.append(jax.ShapeDtypeStruct((n, D_MODEL), F32))
        scratch = [pltpu.VMEM((GATHER_SLOTS, TOKEN_TILE, D_MODEL), F32),
                   pltpu.SemaphoreType.DMA((GATHER_SLOTS,))]
    outs = pl.pallas_call(
        functools.partial(_qkv_kernel, q_scale=q_scale, v_transposed=wvt_bf16 is not None,
                          pending=pending is not None),
        grid_spec=pltpu.PrefetchScalarGridSpec(
            num_scalar_prefetch=len(prefetch), grid=(n // TOKEN_TILE,),
            in_specs=in_specs, out_specs=out_specs, scratch_shapes=scratch),
        out_shape=out_shape,
        compiler_params=_params(1),
        name="qkv_proj",
    )(*prefetch, x, mods, g, w_bf16, wv, qg, kg, bd, *extra)
    return (*outs[:3], outs[3] if pending is not None else x)


def _pair_attention(q2, score_fn, value_fn):
    lane = lax.broadcasted_iota(jnp.int32, q2.shape, 1)
    first = lane < HEAD_DIM
    outs = []
    for keep in (first, jnp.logical_not(first)):
        s = score_fn(jnp.where(keep, q2, jnp.zeros_like(q2)))
        m = functools.reduce(jnp.maximum, [jnp.max(b, axis=-1, keepdims=True) for b in s])
        p = [jnp.exp(b - m) for b in s]
        l = functools.reduce(jnp.add, [jnp.sum(b, axis=-1, keepdims=True) for b in p])
        outs.append(value_fn([b.astype(BF16) for b in p]) / l)
    return jnp.where(first, outs[0], outs[1])


def _ctx_attn_kernel(x_ref, mod_ref, q_ref, k_ref, v_ref, wo_ref, o_ref, att_ref):
    for hp in range(N_HEADS // 2):
        cols = slice(hp * LANES, (hp + 1) * LANES)
        k2 = k_ref[:, cols].astype(BF16)
        v2 = v_ref[:, cols].astype(BF16)
        att = _pair_attention(
            q_ref[:, cols],
            lambda q: [_dot_nt(q, k2)],
            lambda p: jnp.dot(p[0], v2, preferred_element_type=F32))
        att_ref[:, cols] = att.astype(BF16)
    y = jnp.dot(att_ref[...], wo_ref[...], preferred_element_type=F32)
    o_ref[...] = x_ref[...] + mod_ref[...][2:3] * y


def _ctx_attention(x, mods, layer, q, k, v, wo_bf16, seq):
    n = x.shape[0]
    row = lambda i: (i, 0)
    tok = pl.BlockSpec((seq, D_MODEL), row)
    return pl.pallas_call(
        _ctx_attn_kernel,
        grid=(n // seq,),
        in_specs=[tok, _mod_spec(layer, None), tok, tok, tok,
                  pl.BlockSpec((D_MODEL, D_MODEL), lambda i: (0, 0))],
        out_specs=tok,
        out_shape=jax.ShapeDtypeStruct((n, D_MODEL), F32),
        scratch_shapes=[pltpu.VMEM((seq, D_MODEL), BF16)],
        compiler_params=_params(1),
        name="context_attention",
    )(x, mods, q, k, v, wo_bf16)


def _na_kernel(x_ref, mod_ref, q_ref, kp_ref, kc_ref, kn_ref, vp_ref, vc_ref, vn_ref,
               kctx_ref, vctx_ref, bias_ref, wo_ref, o_ref, att_ref, *, n_grid_rows):
    rb = pl.program_id(1)
    entry = []
    for s in range(NA_SLOTS):
        kr = (rb - 1) * NA_ROWS + s
        per_row = []
        for i in range(NA_ROWS):
            r = rb * NA_ROWS + i
            rs = jnp.clip(r - WIN_ROWS // 2, 0, n_grid_rows - WIN_ROWS)
            in_band = jnp.logical_and(kr >= rs, kr < rs + WIN_ROWS)
            per_row.append(jnp.where(in_band, s - i + (WIN_ROWS - 1 - NA_ROWS), BIAS_NONE))
        entry.append(per_row)

    lane = lax.broadcasted_iota(jnp.int32, (GRID_W, LANES), 1)
    left = lane < GRID_W
    qlane = lax.broadcasted_iota(jnp.int32, (NA_TOKENS, LANES), 1)
    k_blocks = (kp_ref, kc_ref, kn_ref)
    v_blocks = (vp_ref, vc_ref, vn_ref)
    for hp in range(N_HEADS // 2):
        cols = slice(hp * LANES, (hp + 1) * LANES)
        q2 = q_ref[:, cols]
        halves = []
        for sub in range(2):
            head = 2 * hp + sub
            keep = (qlane < HEAD_DIM) if sub == 0 else (qlane >= HEAD_DIM)
            qm = jnp.where(keep, q2, jnp.zeros_like(q2))
            s_blocks = []
            for j in range(3):
                st = _dot_nt(k_blocks[j][:, cols], qm)
                rows = []
                for sr in range(NA_ROWS):
                    s = j * NA_ROWS + sr
                    bias = jnp.concatenate(
                        [jnp.where(left, bias_ref[head, entry[s][2 * t]],
                                   bias_ref[head, entry[s][2 * t + 1]])
                         for t in range(NA_ROWS // 2)], axis=-1)
                    rows.append(st[sr * GRID_W:(sr + 1) * GRID_W, :] + bias)
                s_blocks.append(jnp.concatenate(rows, axis=0))
            s_blocks.append(_dot_nt(kctx_ref[:, cols], qm))
            m = functools.reduce(jnp.maximum,
                                 [jnp.max(b, axis=0, keepdims=True) for b in s_blocks])
            p = [jnp.exp2(b - m).astype(BF16) for b in s_blocks]
            vrows = slice(hp * LANES + sub * HEAD_DIM, hp * LANES + (sub + 1) * HEAD_DIM)
            vt = [jnp.concatenate([vb[vrows, :], jnp.ones((ONES_ROWS, vb.shape[1]), BF16)],
                                  axis=0) for vb in v_blocks + (vctx_ref,)]
            ot = functools.reduce(jnp.add, [
                jnp.dot(vt[j], p[j], preferred_element_type=F32) for j in range(4)])
            halves.append(ot[:HEAD_DIM] / ot[HEAD_DIM:HEAD_DIM + 1])
        att_ref[:, cols] = jnp.concatenate(halves, axis=0).T.astype(BF16)
    y = jnp.dot(att_ref[...], wo_ref[...], preferred_element_type=F32)
    o_ref[...] = x_ref[...] + mod_ref[...][2:3] * y


def _neighbourhood_attention(x, mods, layer, q, k, vt, kctx, vctx_t, bias, wo_bf16, batch, seq):
    n_grid_rows = seq // GRID_W
    nrb = n_grid_rows // NA_ROWS
    cur = lambda b, r: (b * nrb + r, 0)
    prev = lambda b, r: (b * nrb + jnp.maximum(r - 1, 0), 0)
    nxt = lambda b, r: (b * nrb + jnp.minimum(r + 1, nrb - 1), 0)
    blk = lambda f: pl.BlockSpec((NA_TOKENS, D_MODEL), f)
    blk_t = lambda f: pl.BlockSpec((None, D_MODEL, NA_TOKENS),
                                   lambda b, r: (f(b, r)[0], 0, 0))
    n_ctx = kctx.shape[1]
    return pl.pallas_call(
        functools.partial(_na_kernel, n_grid_rows=n_grid_rows),
        grid=(batch, nrb),
        in_specs=[blk(cur),
                  pl.BlockSpec((None, None, 6, D_MODEL), lambda b, r: (layer, 1 + b, 0, 0)),
                  blk(cur), blk(prev), blk(cur), blk(nxt),
                  blk_t(prev), blk_t(cur), blk_t(nxt),
                  pl.BlockSpec((None, n_ctx, D_MODEL), lambda b, r: (b, 0, 0)),
                  pl.BlockSpec((None, D_MODEL, n_ctx), lambda b, r: (b, 0, 0)),
                  pl.BlockSpec(bias.shape, lambda b, r: (0, 0, 0, 0)),
                  pl.BlockSpec((D_MODEL, D_MODEL), lambda b, r: (0, 0))],
        out_specs=blk(cur),
        out_shape=jax.ShapeDtypeStruct(x.shape, F32),
        scratch_shapes=[pltpu.VMEM((NA_TOKENS, D_MODEL), BF16)],
        compiler_params=_params(2),
        name="neighbourhood_attention",
    )(x, mods, q, k, k, k, vt, vt, vt, kctx, vctx_t, bias, wo_bf16)


def _na_bias_table(rpb_layer):
    qc = np.arange(GRID_W)[None, :]
    kc = np.arange(GRID_W)[:, None]
    start = np.clip(qc - WIN_COLS // 2, 0, GRID_W - WIN_COLS)
    valid = (kc >= start) & (kc < start + WIN_COLS)
    off = kc - qc + WIN_COLS - 1
    select = ((off[None] == np.arange(2 * WIN_COLS - 1)[:, None, None]) & valid[None])
    m = jnp.einsum('hro,okq->hrkq', rpb_layer * LOG2_E, jnp.asarray(select, F32),
                   precision=lax.Precision.HIGHEST)
    m = m + jnp.asarray(np.where(valid, 0.0, NEG), F32)
    m = jnp.concatenate([m, jnp.full_like(m[:, :1], NEG)], axis=1)
    return jnp.concatenate([m, m], axis=-1)


def _pool_kernel(*refs, tile, seq, n_tokens, pending):
    if pending:
        (dest_ref, x_ref, xp_ref, xn_ref, mod_ref, g_ref, w_ref, ps_ref, pmod_ref, y_hbm,
         o_ref, h_ref, s_ref, ybuf, sems) = refs

        def index_of(step, r):
            pos = step * tile + (r - POOL_HALO)
            inside = POOL_HALO <= r < POOL_HALO + tile
            return pos if inside else jnp.clip(pos, 0, n_tokens - 1)

        y, drain = _gathered_rows(dest_ref, y_hbm, ybuf, sems, index_of)
        gate = pmod_ref[...][5:6]
        x_prev = xp_ref[...] + gate * y[:POOL_HALO]
        x_cur = x_ref[...] + gate * y[POOL_HALO:POOL_HALO + tile]
        x_next = xn_ref[...] + gate * y[POOL_HALO + tile:]
    else:
        (x_ref, xp_ref, xn_ref, mod_ref, g_ref, w_ref, ps_ref, o_ref, h_ref, s_ref) = refs
        x_prev, x_cur, x_next = xp_ref[...], x_ref[...], xn_ref[...]
        drain = lambda: None
    i = pl.program_id(0)
    tiles_per_seq = seq // tile
    t_in_seq = i % tiles_per_seq
    m = mod_ref[...]
    g = g_ref[...]
    h_cur = _rms_modulate(x_cur, g, m[0:1], m[1:2])
    h_prev = _rms_modulate(x_prev, g, m[0:1], m[1:2])
    h_next = _rms_modulate(x_next, g, m[0:1], m[1:2])
    h_ref[0:POOL_HALO, :] = jnp.where(t_in_seq > 0, h_prev, 0.0)
    h_ref[POOL_HALO:POOL_HALO + tile, :] = h_cur
    h_ref[POOL_HALO + tile:, :] = jnp.where(t_in_seq < tiles_per_seq - 1, h_next, 0.0)

    pos = t_in_seq * tile + lax.broadcasted_iota(jnp.int32, (tile, 1), 0)
    ys = []
    for grp, w in enumerate(POOL_SIZES):
        cols = slice(grp * POOL_GROUP_DIM, (grp + 1) * POOL_GROUP_DIM)
        cur, span, valid = h_ref, 1, tile + 2 * POOL_HALO
        while span < w // 2:
            s_ref[0:valid - span, cols] = cur[0:valid - span, cols] + cur[span:valid, cols]
            cur, valid, span = s_ref, valid - span, 2 * span
        total = (cur[POOL_HALO - span:POOL_HALO - span + tile, cols]
                 + cur[POOL_HALO:POOL_HALO + tile, cols])
        lo = jnp.clip(pos - w // 2, 0, seq)
        hi = jnp.clip(pos - w // 2 + w, 0, seq)
        pooled = total / (hi - lo).astype(F32)
        diff = (pooled - h_cur[:, cols]).astype(BF16)
        ys.append(jnp.dot(diff, w_ref[grp], preferred_element_type=F32))
    y_mix = jnp.concatenate(ys, axis=-1) * ps_ref[...]
    o_ref[...] = x_cur + m[2:3] * y_mix
    drain()


def _pool_mixer(x, mods, layer, tiles_per_cond_row, g, w_pool_bf16, pool_scale, tile, seq,
                pending):
    n = x.shape[0]
    hb = tile // POOL_HALO
    last = n // POOL_HALO - 1
    const = lambda i, *_: (0, 0)
    in_specs = [pl.BlockSpec((tile, D_MODEL), lambda i, *_: (i, 0)),
                pl.BlockSpec((POOL_HALO, D_MODEL),
                             lambda i, *_: (jnp.maximum(i * hb - 1, 0), 0)),
                pl.BlockSpec((POOL_HALO, D_MODEL),
                             lambda i, *_: (jnp.minimum((i + 1) * hb, last), 0)),
                _mod_spec(layer, tiles_per_cond_row),
                pl.BlockSpec((1, D_MODEL), const),
                pl.BlockSpec(w_pool_bf16.shape, lambda i, *_: (0, 0, 0)),
                pl.BlockSpec((1, D_MODEL), const)]
    scratch = [pltpu.VMEM((tile + 2 * POOL_HALO, D_MODEL), F32)] * 2
    prefetch, extra = (), ()
    if pending is not None:
        prefetch, extra, extra_specs = _pending_args(pending, mods, tiles_per_cond_row)
        in_specs += extra_specs
        scratch += [pltpu.VMEM((GATHER_SLOTS, tile + 2 * POOL_HALO, D_MODEL), F32),
                    pltpu.SemaphoreType.DMA((GATHER_SLOTS,))]
    return pl.pallas_call(
        functools.partial(_pool_kernel, tile=tile, seq=seq, n_tokens=n,
                          pending=pending is not None),
        grid_spec=pltpu.PrefetchScalarGridSpec(
            num_scalar_prefetch=len(prefetch), grid=(n // tile,),
            in_specs=in_specs,
            out_specs=pl.BlockSpec((tile, D_MODEL), lambda i, *_: (i, 0)),
            scratch_shapes=scratch),
        out_shape=jax.ShapeDtypeStruct(x.shape, F32),
        compiler_params=_params(1),
        name="pool_mixer",
    )(*prefetch, x, x, x, mods, g, w_pool_bf16, pool_scale, *extra)


def _first_argmax(vals):
    best = functools.reduce(jnp.maximum, vals)
    idx = jnp.full(best.shape, len(vals) - 1, jnp.int32)
    for j in range(len(vals) - 2, -1, -1):
        idx = jnp.where(vals[j] == best, j, idx)
    return best, idx


def _router_kernel(x_ref, mod_ref, g_ref, wr_hi_ref, wr_lo_ref, br_ref, aux_ref):
    m = mod_ref[...]
    h = _rms_modulate(x_ref[...], g_ref[...], m[3:4], m[4:5])
    h_hi = h.astype(BF16)
    h_lo = (h - h_hi.astype(F32)).astype(BF16)
    w_hi = wr_hi_ref[...]
    both = _dot_nt(jnp.concatenate([w_hi, wr_lo_ref[...]], axis=0), h_hi)
    logits = (both[:ROUTER_ROWS] + _dot_nt(w_hi, h_lo) + both[ROUTER_ROWS:]) + br_ref[...]
    row = lambda r: logits[r:r + 1, :]
    grp = [row(j) for j in range(N_EXPERT_GROUPS)]
    g_max, g_sel = _first_argmax(grp)
    g_w = 1.0 / functools.reduce(jnp.add, [jnp.exp(v - g_max) for v in grp])
    e_in = []
    for j in range(EXPERTS_PER_GROUP):
        v = row(N_EXPERT_GROUPS + (N_EXPERT_GROUPS - 1) * EXPERTS_PER_GROUP + j)
        for gi in range(N_EXPERT_GROUPS - 2, -1, -1):
            v = jnp.where(g_sel == gi, row(N_EXPERT_GROUPS + gi * EXPERTS_PER_GROUP + j), v)
        e_in.append(v)
    v1, i1 = _first_argmax(e_in)
    v2, i2 = _first_argmax([jnp.where(i1 == j, -jnp.inf, e_in[j])
                            for j in range(EXPERTS_PER_GROUP)])
    t = jnp.exp(v2 - v1)
    w1 = g_w / (1.0 + t)
    w2 = g_w * t / (1.0 + t)
    lo = jnp.minimum(i1, i2)
    hi = jnp.maximum(i1, i2)
    w_lo = jnp.where(i1 < i2, w1, w2)
    w_hi = jnp.where(i1 < i2, w2, w1)
    pair = jnp.where(lo == 0, hi - 1, jnp.where(lo == 1, hi + 1, 5))
    bucket = (g_sel * N_PAIRS + pair).astype(F32)
    pad = jnp.zeros((5, bucket.shape[1]), F32)
    aux_ref[...] = jnp.concatenate([bucket, w_lo, w_hi, pad], axis=0)


def _router(x, mods, layer, tiles_per_cond_row, g, wr_hi, wr_lo, br):
    n = x.shape[0]
    const = lambda i: (0, 0)
    tok = pl.BlockSpec((TOKEN_TILE, D_MODEL), lambda i: (i, 0))
    return pl.pallas_call(
        _router_kernel,
        grid=(n // TOKEN_TILE,),
        in_specs=[tok, _mod_spec(layer, tiles_per_cond_row),
                  pl.BlockSpec((1, D_MODEL), const),
                  pl.BlockSpec((ROUTER_ROWS, D_MODEL), const),
                  pl.BlockSpec((ROUTER_ROWS, D_MODEL), const),
                  pl.BlockSpec((ROUTER_ROWS, 1), const)],
        out_specs=pl.BlockSpec((8, TOKEN_TILE), lambda i: (0, i)),
        out_shape=jax.ShapeDtypeStruct((8, n), F32),
        compiler_params=_params(1),
        name="moe_router",
    )(x, mods, g, wr_hi, wr_lo, br)


def _dispatch_kernel(dest_ref, zero_at_ref, x_ref, mod_ref, g_ref, cw_ref, hs_hbm,
                     rowbuf, zbuf, sems, zsem):
    i = pl.program_id(0)
    last = pl.num_programs(0) - 1
    slot = i % 2

    @pl.when(i == 0)
    def _():
        zbuf[...] = jnp.zeros_like(zbuf)

        def clear(row0):
            return pltpu.make_async_copy(
                zbuf, hs_hbm.at[pl.ds(pl.multiple_of(row0, MOE_TILE), MOE_TILE)], zsem)

        n_tiles = hs_hbm.shape[0] // MOE_TILE
        n_used = zero_at_ref[N_BUCKETS]
        for b in range(N_BUCKETS):
            @pl.when(zero_at_ref[b] >= 0)
            def _():
                clear(zero_at_ref[b]).start()
        lax.fori_loop(n_used, n_tiles, lambda t, c: (clear(t * MOE_TILE).start(), c)[1], 0)
        for b in range(N_BUCKETS):
            @pl.when(zero_at_ref[b] >= 0)
            def _():
                clear(zero_at_ref[b]).wait()
        lax.fori_loop(n_used, n_tiles, lambda t, c: (clear(t * MOE_TILE).wait(), c)[1], 0)

    @pl.when(i >= 2)
    def _():
        _wait_rows(hs_hbm, rowbuf.at[slot], sems.at[slot])

    m = mod_ref[...]
    rowbuf[slot, :, :D_MODEL] = _rms_modulate(x_ref[...], g_ref[...], m[3:4], m[4:5])
    rowbuf[slot, :, D_MODEL:] = cw_ref[...]
    for r in range(TOKEN_TILE):
        pltpu.make_async_copy(rowbuf.at[slot].at[pl.ds(r, 1)],
                              hs_hbm.at[pl.ds(dest_ref[i * TOKEN_TILE + r], 1)],
                              sems.at[slot]).start()

    @pl.when(i == last)
    def _():
        @pl.when(i >= 1)
        def _():
            _wait_rows(hs_hbm, rowbuf.at[1 - slot], sems.at[1 - slot])
        _wait_rows(hs_hbm, rowbuf.at[slot], sems.at[slot])


def _dispatch(x, mods, layer, tiles_per_cond_row, g, dest, zero_at, cw_nat, n_pad):
    n = x.shape[0]
    tok = lambda w: pl.BlockSpec((TOKEN_TILE, w), lambda i, *_: (i, 0))
    return pl.pallas_call(
        _dispatch_kernel,
        grid_spec=pltpu.PrefetchScalarGridSpec(
            num_scalar_prefetch=2,
            grid=(n // TOKEN_TILE,),
            in_specs=[tok(D_MODEL), _mod_spec(layer, tiles_per_cond_row),
                      pl.BlockSpec((1, D_MODEL), lambda i, *_: (0, 0)), tok(LANES)],
            out_specs=pl.BlockSpec(memory_space=pl.ANY),
            scratch_shapes=[pltpu.VMEM((2, TOKEN_TILE, ROW_W), F32),
                            pltpu.VMEM((MOE_TILE, ROW_W), F32),
                            pltpu.SemaphoreType.DMA((2,)),
                            pltpu.SemaphoreType.DMA(())]),
        out_shape=jax.ShapeDtypeStruct((n_pad, ROW_W), F32),
        compiler_params=_params(1),
        name="moe_dispatch",
    )(dest, zero_at, x, mods, g, cw_nat)


def _expert_kernel(elo_ref, ehi_ref, nused_ref, hs_ref,
                   wg_lo, wu_lo, wd_lo, wg_hi, wu_hi, wd_hi, o_ref):
    t = pl.program_id(0)

    @pl.when(t < nused_ref[0])
    def _():
        x = hs_ref[:, :D_MODEL].astype(BF16)
        cw = hs_ref[:, D_MODEL:]
        y = None
        for e, (wg, wu, wd) in enumerate(((wg_lo, wu_lo, wd_lo), (wg_hi, wu_hi, wd_hi))):
            a = jnp.dot(x, wg[...], preferred_element_type=F32)
            u = jnp.dot(x, wu[...], preferred_element_type=F32)
            act = (a / (1.0 + jnp.exp(-a))) * u * cw[:, e:e + 1]
            part = jnp.dot(act.astype(BF16), wd[...], preferred_element_type=F32)
            y = part if y is None else y + part
        o_ref[...] = y

    @pl.when(t >= nused_ref[0])
    def _():
        o_ref[...] = jnp.zeros_like(o_ref)


def _experts(hs, e_lo, e_hi, n_used, layer, wg, wu, wd):
    n_pad = hs.shape[0]
    row_in = lambda t, elo, ehi, nu: (jnp.minimum(t, nu[0] - 1), 0)
    row_out = lambda t, *_: (t, 0)
    w_lo = lambda t, elo, ehi, nu: (layer, elo[t], 0, 0)
    w_hi = lambda t, elo, ehi, nu: (layer, ehi[t], 0, 0)
    up = lambda f: pl.BlockSpec((None, None, D_MODEL, D_FF), f)
    down = lambda f: pl.BlockSpec((None, None, D_FF, D_MODEL), f)
    return pl.pallas_call(
        _expert_kernel,
        grid_spec=pltpu.PrefetchScalarGridSpec(
            num_scalar_prefetch=3,
            grid=(n_pad // MOE_TILE,),
            in_specs=[pl.BlockSpec((MOE_TILE, ROW_W), row_in),
                      up(w_lo), up(w_lo), down(w_lo), up(w_hi), up(w_hi), down(w_hi)],
            out_specs=pl.BlockSpec((MOE_TILE, D_MODEL), row_out)),
        out_shape=jax.ShapeDtypeStruct((n_pad, D_MODEL), F32),
        compiler_params=_params(1),
        name="moe_experts",
    )(e_lo, e_hi, n_used, hs, wg, wu, wd, wg, wu, wd)


def _unsort_kernel(dest_ref, x_ref, mod_ref, y_hbm, o_ref, ybuf, sems):
    y, drain = _gathered_rows(dest_ref, y_hbm, ybuf, sems,
                              lambda step, r: step * TOKEN_TILE + r)
    o_ref[...] = x_ref[...] + mod_ref[...][5:6] * y
    drain()


def _unsort_residual(x, mods, tiles_per_cond_row, pending):
    n = x.shape[0]
    tok = pl.BlockSpec((TOKEN_TILE, D_MODEL), lambda i, *_: (i, 0))
    prefetch, extra, extra_specs = _pending_args(pending, mods, tiles_per_cond_row)
    return pl.pallas_call(
        _unsort_kernel,
        grid_spec=pltpu.PrefetchScalarGridSpec(
            num_scalar_prefetch=len(prefetch),
            grid=(n // TOKEN_TILE,),
            in_specs=[tok] + extra_specs,
            out_specs=tok,
            scratch_shapes=[pltpu.VMEM((GATHER_SLOTS, TOKEN_TILE, D_MODEL), F32),
                            pltpu.SemaphoreType.DMA((GATHER_SLOTS,))]),
        out_shape=jax.ShapeDtypeStruct(x.shape, F32),
        compiler_params=_params(1),
        name="moe_unsort_residual",
    )(*prefetch, x, *extra)


_PAIR_LO = (0, 0, 0, 1, 1, 2)
_PAIR_HI = (1, 2, 3, 2, 3, 3)


def _dispatch_plan(aux, n):
    n_tiles = n // MOE_TILE + N_BUCKETS
    bucket = aux[0].astype(jnp.int32)
    onehot = (bucket[:, None] == jnp.arange(N_BUCKETS)[None, :]).astype(jnp.int32)
    csum = jnp.cumsum(onehot, axis=0)
    counts = csum[-1]
    tiles = (counts + MOE_TILE - 1) // MOE_TILE
    tile_end = jnp.cumsum(tiles)
    tile_start = tile_end - tiles
    n_used = tile_end[-1]
    slot0 = tile_start * MOE_TILE - 1
    dest = jnp.sum(onehot * (csum + slot0[None, :]), axis=1).astype(jnp.int32)
    zero_at = jnp.where(tiles > 0, (tile_end - 1) * MOE_TILE, -1)
    zero_at = jnp.concatenate([zero_at, n_used[None]]).astype(jnp.int32)
    t = jnp.minimum(jnp.arange(n_tiles), n_used - 1)
    tile_bucket = jnp.sum((t[:, None] >= tile_end[None, :]).astype(jnp.int32), axis=1)
    grp = tile_bucket // N_PAIRS
    pair = tile_bucket % N_PAIRS
    e_lo = grp * EXPERTS_PER_GROUP + jnp.asarray(_PAIR_LO, jnp.int32)[pair]
    e_hi = grp * EXPERTS_PER_GROUP + jnp.asarray(_PAIR_HI, jnp.int32)[pair]
    cw_nat = jnp.pad(aux[1:3].T, ((0, 0), (0, LANES - 2)))
    return (dest, zero_at, e_lo.astype(jnp.int32), e_hi.astype(jnp.int32),
            n_used.astype(jnp.int32).reshape(1), cw_nat, n_tiles * MOE_TILE)


def _moe(x, mods, layer, tiles_per_cond_row, g, router_w, expert_w):
    aux = _router(x, mods, layer, tiles_per_cond_row, g, *router_w)
    dest, zero_at, e_lo, e_hi, n_used, cw_nat, n_pad = _dispatch_plan(aux, x.shape[0])
    hs = _dispatch(x, mods, layer, tiles_per_cond_row, g, dest, zero_at, cw_nat, n_pad)
    y_sorted = _experts(hs, e_lo, e_hi, n_used, layer, *expert_w)
    return dest, y_sorted, layer


def kernel(x_prompt, x_sample, cache_k, cache_v, c, c_ctx, norm1_g, norm2_g, w_mod, b_mod,
           w_qkv, q_norm_g, k_norm_g, rpb, w_o, w_pool, pool_scale, w_router_group,
           b_router_group, w_router_expert, b_router_expert, w_gate, w_up, w_down):
    bp, lp, _ = x_prompt.shape
    bs, ls, _ = x_sample.shape
    depth = w_mod.shape[0]
    assert bs + 1 <= COND_ROWS and ls % TOKEN_TILE == 0 and (bp * lp) % TOKEN_TILE == 0
    assert lp <= TOKEN_TILE and TOKEN_TILE % lp == 0

    cond = jnp.zeros((COND_ROWS, D_MODEL), F32).at[0].set(c_ctx).at[1:1 + bs].set(c)
    mods = _modulation(cond, w_mod, b_mod)

    xp = x_prompt.reshape(bp * lp, D_MODEL)
    xs = x_sample.reshape(bs * ls, D_MODEL)
    s_tiles = ls // TOKEN_TILE

    head_of = jnp.arange(MXU_DIM) // HEAD_DIM
    ones_bd = (head_of[:, None] == head_of[None, :]).astype(BF16)

    expert_w = (w_gate.astype(BF16), w_up.astype(BF16), w_down.astype(BF16))
    new_k, new_v = [], []
    pend_p = pend_s = None
    for l in range(depth):
        j = l // 2
        g1 = norm1_g[l][None]
        if l % 2 == 0:
            w_bf16 = w_qkv[j].astype(BF16)
            wvt = w_bf16[:, 2 * D_MODEL:].T
            wo_bf16 = w_o[j].astype(BF16)
            qg = jnp.tile(q_norm_g[j], N_HEADS)[None]
            kg = jnp.tile(k_norm_g[j], N_HEADS)[None]
            scale = HEAD_DIM ** -0.5
            qp, kp, vp, xp = _qkv(xp, mods, l, None, g1, w_bf16, None, qg, kg, ones_bd, F32,
                                  scale, pend_p)
            new_k.append(kp.reshape(bp, lp, N_HEADS, HEAD_DIM))
            new_v.append(vp.reshape(bp, lp, N_HEADS, HEAD_DIM))
            xp = _ctx_attention(xp, mods, l, qp, kp, vp, wo_bf16, lp)
            qs, ks, vts, xs = _qkv(xs, mods, l, s_tiles, g1, w_bf16, wvt, qg, kg, ones_bd,
                                   BF16, scale * LOG2_E, pend_s)
            kctx = cache_k[:, j].reshape(bs, -1, D_MODEL).astype(BF16)
            vctx_t = jnp.swapaxes(cache_v[:, j].reshape(bs, -1, D_MODEL), 1, 2).astype(BF16)
            xs = _neighbourhood_attention(xs, mods, l, qs, ks, vts, kctx, vctx_t,
                                          _na_bias_table(rpb[j]), wo_bf16, bs, ls)
        else:
            wp = w_pool[j].astype(BF16)
            ps = pool_scale[j][None]
            xp = _pool_mixer(xp, mods, l, None, g1, wp, ps, lp, lp, pend_p)
            xs = _pool_mixer(xs, mods, l, s_tiles, g1, wp, ps, TOKEN_TILE, ls, pend_s)

        wr = jnp.zeros((ROUTER_ROWS, D_MODEL), F32)
        wr = wr.at[:N_EXPERT_GROUPS].set(w_router_group[l].T)
        wr = wr.at[N_EXPERT_GROUPS:N_EXPERT_GROUPS + N_EXPERTS].set(w_router_expert[l].T)
        wr_hi = wr.astype(BF16)
        wr_lo = (wr - wr_hi.astype(F32)).astype(BF16)
        br = jnp.zeros((ROUTER_ROWS, 1), F32)
        br = br.at[:N_EXPERT_GROUPS, 0].set(b_router_group[l])
        br = br.at[N_EXPERT_GROUPS:N_EXPERT_GROUPS + N_EXPERTS, 0].set(b_router_expert[l])
        router_w = (wr_hi, wr_lo, br)
        g2 = norm2_g[l][None]
        pend_p = _moe(xp, mods, l, None, g2, router_w, expert_w)
        pend_s = _moe(xs, mods, l, s_tiles, g2, router_w, expert_w)

    xp = _unsort_residual(xp, mods, None, pend_p)
    xs = _unsort_residual(xs, mods, s_tiles, pend_s)
    return (xp.reshape(bp, lp, D_MODEL), xs.reshape(bs, ls, D_MODEL),
            jnp.stack(new_k, axis=1), jnp.stack(new_v, axis=1))
```

```python
import functools

import jax
import jax.numpy as jnp
import numpy as np
from jax import lax
from jax.experimental import pallas as pl
from jax.experimental.pallas import tpu as pltpu

D_MODEL = 1024
N_HEADS = 16
HEAD_DIM = D_MODEL // N_HEADS
GRID_W = 64
WIN_ROWS = 8
WIN_COLS = 16
POOL_SIZES = (2, 4, 8, 16)
POOL_GROUP_DIM = D_MODEL // len(POOL_SIZES)
POOL_HALO = 8
N_EXPERT_GROUPS = 4
EXPERTS_PER_GROUP = 4
N_EXPERTS = N_EXPERT_GROUPS * EXPERTS_PER_GROUP
N_PAIRS = 6
N_BUCKETS = N_EXPERT_GROUPS * N_PAIRS
D_FF = D_MODEL // 2
EPS = 1e-6
NEG = -1e30

LANES = 128
MXU_DIM = 256
TOKEN_TILE = 512
MOE_TILE = 256
ROUTER_ROWS = 32
COND_ROWS = 16
VMEM_LIMIT = 56 * 1024 * 1024

NA_ROWS = 4
NA_TOKENS = NA_ROWS * GRID_W
NA_SLOTS = 3 * NA_ROWS
BIAS_NONE = 2 * WIN_ROWS - 1
ONES_ROWS = 16
LOG2_E = 1.4426950408889634
ROW_W = D_MODEL + LANES
GATHER_SLOTS = 3

BF16 = jnp.bfloat16
F32 = jnp.float32


def _params(n_grid_dims, vmem=VMEM_LIMIT):
    return pltpu.CompilerParams(
        dimension_semantics=("arbitrary",) * n_grid_dims, vmem_limit_bytes=vmem)


def _rms_modulate(x, g, shift, scale):
    y = x * lax.rsqrt(jnp.mean(x * x, axis=-1, keepdims=True) + EPS)
    return (y * g) * (1.0 + scale) + shift


def _dot_nt(a, b):
    return lax.dot_general(a, b, (((1,), (1,)), ((), ())), preferred_element_type=F32)


def _mod_kernel(cond_ref, w_ref, b_ref, o_ref):
    c = cond_ref[...]
    s = c / (1.0 + jnp.exp(-c))
    o_ref[...] = jnp.dot(s, w_ref[...], preferred_element_type=F32,
                         precision=lax.Precision.HIGHEST) + b_ref[...]


def _modulation(cond, w_mod, b_mod):
    depth = w_mod.shape[0]
    tn = 1536
    out = pl.pallas_call(
        _mod_kernel,
        grid=(depth, 6 * D_MODEL // tn),
        in_specs=[
            pl.BlockSpec((COND_ROWS, D_MODEL), lambda l, n: (0, 0)),
            pl.BlockSpec((None, D_MODEL, tn), lambda l, n: (l, 0, n)),
            pl.BlockSpec((None, 1, tn), lambda l, n: (l, 0, n)),
        ],
        out_specs=pl.BlockSpec((None, COND_ROWS, tn), lambda l, n: (l, 0, n)),
        out_shape=jax.ShapeDtypeStruct((depth, COND_ROWS, 6 * D_MODEL), F32),
        compiler_params=_params(2),
        name="adaln_modulation",
    )(cond, w_mod, b_mod.reshape(depth, 1, 6 * D_MODEL))
    return out.reshape(depth, COND_ROWS, 6, D_MODEL)


def _mod_spec(layer, tiles_per_cond_row):
    if tiles_per_cond_row is None:
        return pl.BlockSpec((None, None, 6, D_MODEL), lambda i, *_: (layer, 0, 0, 0))
    return pl.BlockSpec((None, None, 6, D_MODEL),
                        lambda i, *_: (layer, 1 + i // tiles_per_cond_row, 0, 0))


def _wait_rows(hbm, buf, sem):
    pltpu.make_async_copy(hbm.at[pl.ds(0, buf.shape[0])], buf, sem).wait()


def _gathered_rows(idx_ref, y_hbm, ybuf, sems, index_of):
    i = pl.program_id(0)
    last = pl.num_programs(0) - 1
    slot = i % GATHER_SLOTS

    def fetch(step, s):
        for r in range(ybuf.shape[1]):
            pltpu.make_async_copy(y_hbm.at[pl.ds(idx_ref[index_of(step, r)], 1)],
                                  ybuf.at[s].at[pl.ds(r, 1)], sems.at[s]).start()

    @pl.when(i == 0)
    def _():
        fetch(0, 0)
        fetch(jnp.minimum(1, last), 1)

    _wait_rows(y_hbm, ybuf.at[slot], sems.at[slot])
    y = ybuf[slot]
    ahead = (i + 2) % GATHER_SLOTS
    fetch(jnp.minimum(i + 2, last), ahead)

    def drain():
        @pl.when(i == last)
        def _():
            for s in (ahead, (i + 1) % GATHER_SLOTS):
                _wait_rows(y_hbm, ybuf.at[s], sems.at[s])

    return y, drain


def _pending_args(pending, mods, tiles_per_cond_row):
    dest, y_sorted, moe_layer = pending
    return ((dest,), (mods, y_sorted),
            [_mod_spec(moe_layer, tiles_per_cond_row), pl.BlockSpec(memory_space=pl.ANY)])


def _head_rms(t, ones_bd, gain):
    sq = (t * t).astype(BF16)
    parts = [jnp.dot(sq[:, c:c + MXU_DIM], ones_bd, preferred_element_type=F32)
             for c in range(0, D_MODEL, MXU_DIM)]
    mean = jnp.concatenate(parts, axis=-1) / HEAD_DIM
    return t * lax.rsqrt(mean + EPS) * gain


def _qkv_kernel(*refs, q_scale, v_transposed, pending):
    if pending:
        (dest_ref, x_ref, mod_ref, g_ref, wqk_ref, wv_ref, qg_ref, kg_ref, bd_ref,
         pmod_ref, y_hbm, q_ref, k_ref, v_ref, xnew_ref, ybuf, sems) = refs
        y, drain = _gathered_rows(dest_ref, y_hbm, ybuf, sems,
                                  lambda step, r: step * TOKEN_TILE + r)
        x = x_ref[...] + pmod_ref[...][5:6] * y
        xnew_ref[...] = x
    else:
        (x_ref, mod_ref, g_ref, wqk_ref, wv_ref, qg_ref, kg_ref, bd_ref,
         q_ref, k_ref, v_ref) = refs
        x, drain = x_ref[...], lambda: None
    m = mod_ref[...]
    h = _rms_modulate(x, g_ref[...], m[0:1], m[1:2]).astype(BF16)
    qk = jnp.dot(h, wqk_ref[...], preferred_element_type=F32)
    bd = bd_ref[...]
    q = _head_rms(qk[:, :D_MODEL], bd, qg_ref[...])
    k = _head_rms(qk[:, D_MODEL:], bd, kg_ref[...])
    q_ref[...] = (q * q_scale).astype(q_ref.dtype)
    k_ref[...] = k.astype(k_ref.dtype)
    if v_transposed:
        vt = _dot_nt(wv_ref[...], h).astype(v_ref.dtype)
        for j in range(TOKEN_TILE // NA_TOKENS):
            v_ref[j] = vt[:, j * NA_TOKENS:(j + 1) * NA_TOKENS]
    else:
        v_ref[...] = jnp.dot(h, wv_ref[...], preferred_element_type=F32).astype(v_ref.dtype)
    drain()


def _qkv(x, mods, layer, tiles_per_cond_row, g, w_bf16, wvt_bf16, qg, kg, bd, kv_dtype,
         q_scale, pending):
    n = x.shape[0]
    row = lambda i, *_: (i, 0)
    const = lambda i, *_: (0, 0)
    tok = pl.BlockSpec((TOKEN_TILE, D_MODEL), row)
    vec = pl.BlockSpec((1, D_MODEL), const)
    if wvt_bf16 is not None:
        per_tile = TOKEN_TILE // NA_TOKENS
        v_spec = pl.BlockSpec((per_tile, D_MODEL, NA_TOKENS), lambda i, *_: (i, 0, 0))
        v_shape = jax.ShapeDtypeStruct((n // NA_TOKENS, D_MODEL, NA_TOKENS), kv_dtype)
        wv, wv_spec = wvt_bf16, pl.BlockSpec((D_MODEL, D_MODEL), const)
    else:
        v_spec, v_shape = tok, jax.ShapeDtypeStruct((n, D_MODEL), kv_dtype)
        wv, wv_spec = w_bf16, pl.BlockSpec((D_MODEL, D_MODEL), lambda i, *_: (0, 2))
    in_specs = [tok, _mod_spec(layer, tiles_per_cond_row), vec,
                pl.BlockSpec((D_MODEL, 2 * D_MODEL), const), wv_spec, vec, vec,
                pl.BlockSpec((MXU_DIM, MXU_DIM), const)]
    out_specs = [tok, tok, v_spec]
    out_shape = [jax.ShapeDtypeStruct((n, D_MODEL), BF16),
                 jax.ShapeDtypeStruct((n, D_MODEL), kv_dtype), v_shape]
    prefetch, extra, scratch = (), (), []
    if pending is not None:
        prefetch, extra, extra_specs = _pending_args(pending, mods, tiles_per_cond_row)
        in_specs += extra_specs
        out_specs.append(tok)
        out_shape.append(jax.ShapeDtypeStruct((n, D_MODEL), F32))
        scratch = [pltpu.VMEM((GATHER_SLOTS, TOKEN_TILE, D_MODEL), F32),
                   pltpu.SemaphoreType.DMA((GATHER_SLOTS,))]
    outs = pl.pallas_call(
        functools.partial(_qkv_kernel, q_scale=q_scale, v_transposed=wvt_bf16 is not None,
                          pending=pending is not None),
        grid_spec=pltpu.PrefetchScalarGridSpec(
            num_scalar_prefetch=len(prefetch), grid=(n // TOKEN_TILE,),
            in_specs=in_specs, out_specs=out_specs, scratch_shapes=scratch),
        out_shape=out_shape,
        compiler_params=_params(1),
        name="qkv_proj",
    )(*prefetch, x, mods, g, w_bf16, wv, qg, kg, bd, *extra)
    return (*outs[:3], outs[3] if pending is not None else x)


def _pair_attention(q2, score_fn, value_fn):
    lane = lax.broadcasted_iota(jnp.int32, q2.shape, 1)
    first = lane < HEAD_DIM
    outs = []
    for keep in (first, jnp.logical_not(first)):
        s = score_fn(jnp.where(keep, q2, jnp.zeros_like(q2)))
        m = functools.reduce(jnp.maximum, [jnp.max(b, axis=-1, keepdims=True) for b in s])
        p = [jnp.exp(b - m) for b in s]
        l = functools.reduce(jnp.add, [jnp.sum(b, axis=-1, keepdims=True) for b in p])
        outs.append(value_fn([b.astype(BF16) for b in p]) / l)
    return jnp.where(first, outs[0], outs[1])


def _ctx_attn_kernel(x_ref, mod_ref, q_ref, k_ref, v_ref, wo_ref, o_ref, att_ref):
    for hp in range(N_HEADS // 2):
        cols = slice(hp * LANES, (hp + 1) * LANES)
        k2 = k_ref[:, cols].astype(BF16)
        v2 = v_ref[:, cols].astype(BF16)
        att = _pair_attention(
            q_ref[:, cols],
            lambda q: [_dot_nt(q, k2)],
            lambda p: jnp.dot(p[0], v2, preferred_element_type=F32))
        att_ref[:, cols] = att.astype(BF16)
    y = jnp.dot(att_ref[...], wo_ref[...], preferred_element_type=F32)
    o_ref[...] = x_ref[...] + mod_ref[...][2:3] * y


def _ctx_attention(x, mods, layer, q, k, v, wo_bf16, seq):
    n = x.shape[0]
    row = lambda i: (i, 0)
    tok = pl.BlockSpec((seq, D_MODEL), row)
    return pl.pallas_call(
        _ctx_attn_kernel,
        grid=(n // seq,),
        in_specs=[tok, _mod_spec(layer, None), tok, tok, tok,
                  pl.BlockSpec((D_MODEL, D_MODEL), lambda i: (0, 0))],
        out_specs=tok,
        out_shape=jax.ShapeDtypeStruct((n, D_MODEL), F32),
        scratch_shapes=[pltpu.VMEM((seq, D_MODEL), BF16)],
        compiler_params=_params(1),
        name="context_attention",
    )(x, mods, q, k, v, wo_bf16)


def _na_kernel(x_ref, mod_ref, q_ref, kp_ref, kc_ref, kn_ref, vp_ref, vc_ref, vn_ref,
               kctx_ref, vctx_ref, bias_ref, wo_ref, o_ref, att_ref, *, n_grid_rows):
    rb = pl.program_id(1)
    entry = []
    for s in range(NA_SLOTS):
        kr = (rb - 1) * NA_ROWS + s
        per_row = []
        for i in range(NA_ROWS):
            r = rb * NA_ROWS + i
            rs = jnp.clip(r - WIN_ROWS // 2, 0, n_grid_rows - WIN_ROWS)
            in_band = jnp.logical_and(kr >= rs, kr < rs + WIN_ROWS)
            per_row.append(jnp.where(in_band, s - i + (WIN_ROWS - 1 - NA_ROWS), BIAS_NONE))
        entry.append(per_row)

    lane = lax.broadcasted_iota(jnp.int32, (GRID_W, LANES), 1)
    left = lane < GRID_W
    qlane = lax.broadcasted_iota(jnp.int32, (NA_TOKENS, LANES), 1)
    k_blocks = (kp_ref, kc_ref, kn_ref)
    v_blocks = (vp_ref, vc_ref, vn_ref)
    for hp in range(N_HEADS // 2):
        cols = slice(hp * LANES, (hp + 1) * LANES)
        q2 = q_ref[:, cols]
        halves = []
        for sub in range(2):
            head = 2 * hp + sub
            keep = (qlane < HEAD_DIM) if sub == 0 else (qlane >= HEAD_DIM)
            qm = jnp.where(keep, q2, jnp.zeros_like(q2))
            s_blocks = []
            for j in range(3):
                st = _dot_nt(k_blocks[j][:, cols], qm)
                rows = []
                for sr in range(NA_ROWS):
                    s = j * NA_ROWS + sr
                    bias = jnp.concatenate(
                        [jnp.where(left, bias_ref[head, entry[s][2 * t]],
                                   bias_ref[head, entry[s][2 * t + 1]])
                         for t in range(NA_ROWS // 2)], axis=-1)
                    rows.append(st[sr * GRID_W:(sr + 1) * GRID_W, :] + bias)
                s_blocks.append(jnp.concatenate(rows, axis=0))
            s_blocks.append(_dot_nt(kctx_ref[:, cols], qm))
            m = functools.reduce(jnp.maximum,
                                 [jnp.max(b, axis=0, keepdims=True) for b in s_blocks])
            p = [jnp.exp2(b - m).astype(BF16) for b in s_blocks]
            vrows = slice(hp * LANES + sub * HEAD_DIM, hp * LANES + (sub + 1) * HEAD_DIM)
            vt = [jnp.concatenate([vb[vrows, :], jnp.ones((ONES_ROWS, vb.shape[1]), BF16)],
                                  axis=0) for vb in v_blocks + (vctx_ref,)]
            ot = functools.reduce(jnp.add, [
                jnp.dot(vt[j], p[j], preferred_element_type=F32) for j in range(4)])
            halves.append(ot[:HEAD_DIM] / ot[HEAD_DIM:HEAD_DIM + 1])
        att_ref[:, cols] = jnp.concatenate(halves, axis=0).T.astype(BF16)
    y = jnp.dot(att_ref[...], wo_ref[...], preferred_element_type=F32)
    o_ref[...] = x_ref[...] + mod_ref[...][2:3] * y


def _neighbourhood_attention(x, mods, layer, q, k, vt, kctx, vctx_t, bias, wo_bf16, batch, seq):
    n_grid_rows = seq // GRID_W
    nrb = n_grid_rows // NA_ROWS
    cur = lambda b, r: (b * nrb + r, 0)
    prev = lambda b, r: (b * nrb + jnp.maximum(r - 1, 0), 0)
    nxt = lambda b, r: (b * nrb + jnp.minimum(r + 1, nrb - 1), 0)
    blk = lambda f: pl.BlockSpec((NA_TOKENS, D_MODEL), f)
    blk_t = lambda f: pl.BlockSpec((None, D_MODEL, NA_TOKENS),
                                   lambda b, r: (f(b, r)[0], 0, 0))
    n_ctx = kctx.shape[1]
    return pl.pallas_call(
        functools.partial(_na_kernel, n_grid_rows=n_grid_rows),
        grid=(batch, nrb),
        in_specs=[blk(cur),
                  pl.BlockSpec((None, None, 6, D_MODEL), lambda b, r: (layer, 1 + b, 0, 0)),
                  blk(cur), blk(prev), blk(cur), blk(nxt),
                  blk_t(prev), blk_t(cur), blk_t(nxt),
                  pl.BlockSpec((None, n_ctx, D_MODEL), lambda b, r: (b, 0, 0)),
                  pl.BlockSpec((None, D_MODEL, n_ctx), lambda b, r: (b, 0, 0)),
                  pl.BlockSpec(bias.shape, lambda b, r: (0, 0, 0, 0)),
                  pl.BlockSpec((D_MODEL, D_MODEL), lambda b, r: (0, 0))],
        out_specs=blk(cur),
        out_shape=jax.ShapeDtypeStruct(x.shape, F32),
        scratch_shapes=[pltpu.VMEM((NA_TOKENS, D_MODEL), BF16)],
        compiler_params=_params(2),
        name="neighbourhood_attention",
    )(x, mods, q, k, k, k, vt, vt, vt, kctx, vctx_t, bias, wo_bf16)


def _na_bias_table(rpb_layer):
    qc = np.arange(GRID_W)[None, :]
    kc = np.arange(GRID_W)[:, None]
    start = np.clip(qc - WIN_COLS // 2, 0, GRID_W - WIN_COLS)
    valid = (kc >= start) & (kc < start + WIN_COLS)
    off = kc - qc + WIN_COLS - 1
    select = ((off[None] == np.arange(2 * WIN_COLS - 1)[:, None, None]) & valid[None])
    m = jnp.einsum('hro,okq->hrkq', rpb_layer * LOG2_E, jnp.asarray(select, F32),
                   precision=lax.Precision.HIGHEST)
    m = m + jnp.asarray(np.where(valid, 0.0, NEG), F32)
    m = jnp.concatenate([m, jnp.full_like(m[:, :1], NEG)], axis=1)
    return jnp.concatenate([m, m], axis=-1)


def _pool_kernel(*refs, tile, seq, n_tokens, pending):
    if pending:
        (dest_ref, x_ref, xp_ref, xn_ref, mod_ref, g_ref, w_ref, ps_ref, pmod_ref, y_hbm,
         o_ref, h_ref, s_ref, ybuf, sems) = refs

        def index_of(step, r):
            pos = step * tile + (r - POOL_HALO)
            inside = POOL_HALO <= r < POOL_HALO + tile
            return pos if inside else jnp.clip(pos, 0, n_tokens - 1)

        y, drain = _gathered_rows(dest_ref, y_hbm, ybuf, sems, index_of)
        gate = pmod_ref[...][5:6]
        x_prev = xp_ref[...] + gate * y[:POOL_HALO]
        x_cur = x_ref[...] + gate * y[POOL_HALO:POOL_HALO + tile]
        x_next = xn_ref[...] + gate * y[POOL_HALO + tile:]
    else:
        (x_ref, xp_ref, xn_ref, mod_ref, g_ref, w_ref, ps_ref, o_ref, h_ref, s_ref) = refs
        x_prev, x_cur, x_next = xp_ref[...], x_ref[...], xn_ref[...]
        drain = lambda: None
    i = pl.program_id(0)
    tiles_per_seq = seq // tile
    t_in_seq = i % tiles_per_seq
    m = mod_ref[...]
    g = g_ref[...]
    h_cur = _rms_modulate(x_cur, g, m[0:1], m[1:2])
    h_prev = _rms_modulate(x_prev, g, m[0:1], m[1:2])
    h_next = _rms_modulate(x_next, g, m[0:1], m[1:2])
    h_ref[0:POOL_HALO, :] = jnp.where(t_in_seq > 0, h_prev, 0.0)
    h_ref[POOL_HALO:POOL_HALO + tile, :] = h_cur
    h_ref[POOL_HALO + tile:, :] = jnp.where(t_in_seq < tiles_per_seq - 1, h_next, 0.0)

    pos = t_in_seq * tile + lax.broadcasted_iota(jnp.int32, (tile, 1), 0)
    ys = []
    for grp, w in enumerate(POOL_SIZES):
        cols = slice(grp * POOL_GROUP_DIM, (grp + 1) * POOL_GROUP_DIM)
        cur, span, valid = h_ref, 1, tile + 2 * POOL_HALO
        while span < w // 2:
            s_ref[0:valid - span, cols] = cur[0:valid - span, cols] + cur[span:valid, cols]
            cur, valid, span = s_ref, valid - span, 2 * span
        total = (cur[POOL_HALO - span:POOL_HALO - span + tile, cols]
                 + cur[POOL_HALO:POOL_HALO + tile, cols])
        lo = jnp.clip(pos - w // 2, 0, seq)
        hi = jnp.clip(pos - w // 2 + w, 0, seq)
        pooled = total / (hi - lo).astype(F32)
        diff = (pooled - h_cur[:, cols]).astype(BF16)
        ys.append(jnp.dot(diff, w_ref[grp], preferred_element_type=F32))
    y_mix = jnp.concatenate(ys, axis=-1) * ps_ref[...]
    o_ref[...] = x_cur + m[2:3] * y_mix
    drain()


def _pool_mixer(x, mods, layer, tiles_per_cond_row, g, w_pool_bf16, pool_scale, tile, seq,
                pending):
    n = x.shape[0]
    hb = tile // POOL_HALO
    last = n // POOL_HALO - 1
    const = lambda i, *_: (0, 0)
    in_specs = [pl.BlockSpec((tile, D_MODEL), lambda i, *_: (i, 0)),
                pl.BlockSpec((POOL_HALO, D_MODEL),
                             lambda i, *_: (jnp.maximum(i * hb - 1, 0), 0)),
                pl.BlockSpec((POOL_HALO, D_MODEL),
                             lambda i, *_: (jnp.minimum((i + 1) * hb, last), 0)),
                _mod_spec(layer, tiles_per_cond_row),
                pl.BlockSpec((1, D_MODEL), const),
                pl.BlockSpec(w_pool_bf16.shape, lambda i, *_: (0, 0, 0)),
                pl.BlockSpec((1, D_MODEL), const)]
    scratch = [pltpu.VMEM((tile + 2 * POOL_HALO, D_MODEL), F32)] * 2
    prefetch, extra = (), ()
    if pending is not None:
        prefetch, extra, extra_specs = _pending_args(pending, mods, tiles_per_cond_row)
        in_specs += extra_specs
        scratch += [pltpu.VMEM((GATHER_SLOTS, tile + 2 * POOL_HALO, D_MODEL), F32),
                    pltpu.SemaphoreType.DMA((GATHER_SLOTS,))]
    return pl.pallas_call(
        functools.partial(_pool_kernel, tile=tile, seq=seq, n_tokens=n,
                          pending=pending is not None),
        grid_spec=pltpu.PrefetchScalarGridSpec(
            num_scalar_prefetch=len(prefetch), grid=(n // tile,),
            in_specs=in_specs,
            out_specs=pl.BlockSpec((tile, D_MODEL), lambda i, *_: (i, 0)),
            scratch_shapes=scratch),
        out_shape=jax.ShapeDtypeStruct(x.shape, F32),
        compiler_params=_params(1),
        name="pool_mixer",
    )(*prefetch, x, x, x, mods, g, w_pool_bf16, pool_scale, *extra)


def _first_argmax(vals):
    best = functools.reduce(jnp.maximum, vals)
    idx = jnp.full(best.shape, len(vals) - 1, jnp.int32)
    for j in range(len(vals) - 2, -1, -1):
        idx = jnp.where(vals[j] == best, j, idx)
    return best, idx


def _router_kernel(x_ref, mod_ref, g_ref, wr_hi_ref, wr_lo_ref, br_ref, aux_ref, cw_ref):
    m = mod_ref[...]
    h = _rms_modulate(x_ref[...], g_ref[...], m[3:4], m[4:5])
    h_hi = h.astype(BF16)
    h_lo = (h - h_hi.astype(F32)).astype(BF16)
    w_hi = wr_hi_ref[...]
    both = _dot_nt(jnp.concatenate([w_hi, wr_lo_ref[...]], axis=0), h_hi)
    logits = (both[:ROUTER_ROWS] + _dot_nt(w_hi, h_lo) + both[ROUTER_ROWS:]) + br_ref[...]
    row = lambda r: logits[r:r + 1, :]
    grp = [row(j) for j in range(N_EXPERT_GROUPS)]
    g_max, g_sel = _first_argmax(grp)
    g_w = 1.0 / functools.reduce(jnp.add, [jnp.exp(v - g_max) for v in grp])
    e_in = []
    for j in range(EXPERTS_PER_GROUP):
        v = row(N_EXPERT_GROUPS + (N_EXPERT_GROUPS - 1) * EXPERTS_PER_GROUP + j)
        for gi in range(N_EXPERT_GROUPS - 2, -1, -1):
            v = jnp.where(g_sel == gi, row(N_EXPERT_GROUPS + gi * EXPERTS_PER_GROUP + j), v)
        e_in.append(v)
    v1, i1 = _first_argmax(e_in)
    v2, i2 = _first_argmax([jnp.where(i1 == j, -jnp.inf, e_in[j])
                            for j in range(EXPERTS_PER_GROUP)])
    t = jnp.exp(v2 - v1)
    w1 = g_w / (1.0 + t)
    w2 = g_w * t / (1.0 + t)
    lo = jnp.minimum(i1, i2)
    hi = jnp.maximum(i1, i2)
    w_lo = jnp.where(i1 < i2, w1, w2)
    w_hi = jnp.where(i1 < i2, w2, w1)
    pair = jnp.where(lo == 0, hi - 1, jnp.where(lo == 1, hi + 1, 5))
    bucket = (g_sel * N_PAIRS + pair).astype(F32)
    n_tok = bucket.shape[1]
    aux_ref[...] = jnp.concatenate([bucket, jnp.zeros((7, n_tok), F32)], axis=0)
    cw_ref[...] = jnp.concatenate([w_lo, w_hi, jnp.zeros((LANES - 2, n_tok), F32)], axis=0).T


def _router(x, mods, layer, tiles_per_cond_row, g, wr_hi, wr_lo, br):
    n = x.shape[0]
    const = lambda i: (0, 0)
    tok = pl.BlockSpec((TOKEN_TILE, D_MODEL), lambda i: (i, 0))
    return pl.pallas_call(
        _router_kernel,
        grid=(n // TOKEN_TILE,),
        in_specs=[tok, _mod_spec(layer, tiles_per_cond_row),
                  pl.BlockSpec((1, D_MODEL), const),
                  pl.BlockSpec((ROUTER_ROWS, D_MODEL), const),
                  pl.BlockSpec((ROUTER_ROWS, D_MODEL), const),
                  pl.BlockSpec((ROUTER_ROWS, 1), const)],
        out_specs=[pl.BlockSpec((8, TOKEN_TILE), lambda i: (0, i)),
                   pl.BlockSpec((TOKEN_TILE, LANES), lambda i: (i, 0))],
        out_shape=[jax.ShapeDtypeStruct((8, n), F32), jax.ShapeDtypeStruct((n, LANES), F32)],
        compiler_params=_params(1),
        name="moe_router",
    )(x, mods, g, wr_hi, wr_lo, br)


def _dispatch_kernel(dest_ref, zero_at_ref, x_ref, mod_ref, g_ref, cw_ref, hs_hbm,
                     rowbuf, zbuf, sems, zsem):
    i = pl.program_id(0)
    last = pl.num_programs(0) - 1
    slot = i % 2

    @pl.when(i == 0)
    def _():
        zbuf[...] = jnp.zeros_like(zbuf)

        def clear(row0):
            return pltpu.make_async_copy(
                zbuf, hs_hbm.at[pl.ds(pl.multiple_of(row0, MOE_TILE), MOE_TILE)], zsem)

        n_tiles = hs_hbm.shape[0] // MOE_TILE
        n_used = zero_at_ref[N_BUCKETS]
        for b in range(N_BUCKETS):
            @pl.when(zero_at_ref[b] >= 0)
            def _():
                clear(zero_at_ref[b]).start()
        lax.fori_loop(n_used, n_tiles, lambda t, c: (clear(t * MOE_TILE).start(), c)[1], 0)
        for b in range(N_BUCKETS):
            @pl.when(zero_at_ref[b] >= 0)
            def _():
                clear(zero_at_ref[b]).wait()
        lax.fori_loop(n_used, n_tiles, lambda t, c: (clear(t * MOE_TILE).wait(), c)[1], 0)

    @pl.when(i >= 2)
    def _():
        _wait_rows(hs_hbm, rowbuf.at[slot], sems.at[slot])

    m = mod_ref[...]
    rowbuf[slot, :, :D_MODEL] = _rms_modulate(x_ref[...], g_ref[...], m[3:4], m[4:5])
    rowbuf[slot, :, D_MODEL:] = cw_ref[...]
    for r in range(TOKEN_TILE):
        pltpu.make_async_copy(rowbuf.at[slot].at[pl.ds(r, 1)],
                              hs_hbm.at[pl.ds(dest_ref[i * TOKEN_TILE + r], 1)],
                              sems.at[slot]).start()

    @pl.when(i == last)
    def _():
        @pl.when(i >= 1)
        def _():
            _wait_rows(hs_hbm, rowbuf.at[1 - slot], sems.at[1 - slot])
        _wait_rows(hs_hbm, rowbuf.at[slot], sems.at[slot])


def _dispatch(x, mods, layer, tiles_per_cond_row, g, dest, zero_at, cw_nat, n_pad):
    n = x.shape[0]
    tok = lambda w: pl.BlockSpec((TOKEN_TILE, w), lambda i, *_: (i, 0))
    return pl.pallas_call(
        _dispatch_kernel,
        grid_spec=pltpu.PrefetchScalarGridSpec(
            num_scalar_prefetch=2,
            grid=(n // TOKEN_TILE,),
            in_specs=[tok(D_MODEL), _mod_spec(layer, tiles_per_cond_row),
                      pl.BlockSpec((1, D_MODEL), lambda i, *_: (0, 0)), tok(LANES)],
            out_specs=pl.BlockSpec(memory_space=pl.ANY),
            scratch_shapes=[pltpu.VMEM((2, TOKEN_TILE, ROW_W), F32),
                            pltpu.VMEM((MOE_TILE, ROW_W), F32),
                            pltpu.SemaphoreType.DMA((2,)),
                            pltpu.SemaphoreType.DMA(())]),
        out_shape=jax.ShapeDtypeStruct((n_pad, ROW_W), F32),
        compiler_params=_params(1),
        name="moe_dispatch",
    )(dest, zero_at, x, mods, g, cw_nat)


def _expert_kernel(elo_ref, ehi_ref, nused_ref, hs_ref,
                   wg_lo, wu_lo, wd_lo, wg_hi, wu_hi, wd_hi, o_ref):
    t = pl.program_id(0)

    @pl.when(t < nused_ref[0])
    def _():
        x = hs_ref[:, :D_MODEL].astype(BF16)
        cw = hs_ref[:, D_MODEL:]
        y = None
        for e, (wg, wu, wd) in enumerate(((wg_lo, wu_lo, wd_lo), (wg_hi, wu_hi, wd_hi))):
            a = jnp.dot(x, wg[...], preferred_element_type=F32)
            u = jnp.dot(x, wu[...], preferred_element_type=F32)
            act = (a / (1.0 + jnp.exp(-a))) * u * cw[:, e:e + 1]
            part = jnp.dot(act.astype(BF16), wd[...], preferred_element_type=F32)
            y = part if y is None else y + part
        o_ref[...] = y

    @pl.when(t >= nused_ref[0])
    def _():
        o_ref[...] = jnp.zeros_like(o_ref)


def _experts(hs, e_lo, e_hi, n_used, layer, wg, wu, wd):
    n_pad = hs.shape[0]
    row_in = lambda t, elo, ehi, nu: (jnp.minimum(t, nu[0] - 1), 0)
    row_out = lambda t, *_: (t, 0)
    w_lo = lambda t, elo, ehi, nu: (layer, elo[t], 0, 0)
    w_hi = lambda t, elo, ehi, nu: (layer, ehi[t], 0, 0)
    up = lambda f: pl.BlockSpec((None, None, D_MODEL, D_FF), f)
    down = lambda f: pl.BlockSpec((None, None, D_FF, D_MODEL), f)
    return pl.pallas_call(
        _expert_kernel,
        grid_spec=pltpu.PrefetchScalarGridSpec(
            num_scalar_prefetch=3,
            grid=(n_pad // MOE_TILE,),
            in_specs=[pl.BlockSpec((MOE_TILE, ROW_W), row_in),
                      up(w_lo), up(w_lo), down(w_lo), up(w_hi), up(w_hi), down(w_hi)],
            out_specs=pl.BlockSpec((MOE_TILE, D_MODEL), row_out)),
        out_shape=jax.ShapeDtypeStruct((n_pad, D_MODEL), F32),
        compiler_params=_params(1),
        name="moe_experts",
    )(e_lo, e_hi, n_used, hs, wg, wu, wd, wg, wu, wd)


def _unsort_kernel(dest_ref, x_ref, mod_ref, y_hbm, o_ref, ybuf, sems):
    y, drain = _gathered_rows(dest_ref, y_hbm, ybuf, sems,
                              lambda step, r: step * TOKEN_TILE + r)
    o_ref[...] = x_ref[...] + mod_ref[...][5:6] * y
    drain()


def _unsort_residual(x, mods, tiles_per_cond_row, pending):
    n = x.shape[0]
    tok = pl.BlockSpec((TOKEN_TILE, D_MODEL), lambda i, *_: (i, 0))
    prefetch, extra, extra_specs = _pending_args(pending, mods, tiles_per_cond_row)
    return pl.pallas_call(
        _unsort_kernel,
        grid_spec=pltpu.PrefetchScalarGridSpec(
            num_scalar_prefetch=len(prefetch),
            grid=(n // TOKEN_TILE,),
            in_specs=[tok] + extra_specs,
            out_specs=tok,
            scratch_shapes=[pltpu.VMEM((GATHER_SLOTS, TOKEN_TILE, D_MODEL), F32),
                            pltpu.SemaphoreType.DMA((GATHER_SLOTS,))]),
        out_shape=jax.ShapeDtypeStruct(x.shape, F32),
        compiler_params=_params(1),
        name="moe_unsort_residual",
    )(*prefetch, x, *extra)


_PAIR_LO = (0, 0, 0, 1, 1, 2)
_PAIR_HI = (1, 2, 3, 2, 3, 3)


def _dispatch_plan(aux, n):
    n_tiles = n // MOE_TILE + N_BUCKETS
    bucket = aux[0].astype(jnp.int32)
    onehot = (bucket[:, None] == jnp.arange(N_BUCKETS)[None, :]).astype(jnp.int32)
    csum = jnp.cumsum(onehot, axis=0)
    counts = csum[-1]
    tiles = (counts + MOE_TILE - 1) // MOE_TILE
    tile_end = jnp.cumsum(tiles)
    tile_start = tile_end - tiles
    n_used = tile_end[-1]
    slot0 = tile_start * MOE_TILE - 1
    dest = jnp.sum(onehot * (csum + slot0[None, :]), axis=1).astype(jnp.int32)
    zero_at = jnp.where(tiles > 0, (tile_end - 1) * MOE_TILE, -1)
    zero_at = jnp.concatenate([zero_at, n_used[None]]).astype(jnp.int32)
    t = jnp.minimum(jnp.arange(n_tiles), n_used - 1)
    tile_bucket = jnp.sum((t[:, None] >= tile_end[None, :]).astype(jnp.int32), axis=1)
    grp = tile_bucket // N_PAIRS
    pair = tile_bucket % N_PAIRS
    e_lo = grp * EXPERTS_PER_GROUP + jnp.asarray(_PAIR_LO, jnp.int32)[pair]
    e_hi = grp * EXPERTS_PER_GROUP + jnp.asarray(_PAIR_HI, jnp.int32)[pair]
    return (dest, zero_at, e_lo.astype(jnp.int32), e_hi.astype(jnp.int32),
            n_used.astype(jnp.int32).reshape(1), n_tiles * MOE_TILE)


def _moe(x, mods, layer, tiles_per_cond_row, g, router_w, expert_w):
    aux, cw_nat = _router(x, mods, layer, tiles_per_cond_row, g, *router_w)
    dest, zero_at, e_lo, e_hi, n_used, n_pad = _dispatch_plan(aux, x.shape[0])
    hs = _dispatch(x, mods, layer, tiles_per_cond_row, g, dest, zero_at, cw_nat, n_pad)
    y_sorted = _experts(hs, e_lo, e_hi, n_used, layer, *expert_w)
    return dest, y_sorted, layer


def kernel(x_prompt, x_sample, cache_k, cache_v, c, c_ctx, norm1_g, norm2_g, w_mod, b_mod,
           w_qkv, q_norm_g, k_norm_g, rpb, w_o, w_pool, pool_scale, w_router_group,
           b_router_group, w_router_expert, b_router_expert, w_gate, w_up, w_down):
    bp, lp, _ = x_prompt.shape
    bs, ls, _ = x_sample.shape
    depth = w_mod.shape[0]
    assert bs + 1 <= COND_ROWS and ls % TOKEN_TILE == 0 and (bp * lp) % TOKEN_TILE == 0
    assert lp <= TOKEN_TILE and TOKEN_TILE % lp == 0

    cond = jnp.zeros((COND_ROWS, D_MODEL), F32).at[0].set(c_ctx).at[1:1 + bs].set(c)
    mods = _modulation(cond, w_mod, b_mod)

    xp = x_prompt.reshape(bp * lp, D_MODEL)
    xs = x_sample.reshape(bs * ls, D_MODEL)
    s_tiles = ls // TOKEN_TILE

    head_of = jnp.arange(MXU_DIM) // HEAD_DIM
    ones_bd = (head_of[:, None] == head_of[None, :]).astype(BF16)

    expert_w = (w_gate.astype(BF16), w_up.astype(BF16), w_down.astype(BF16))
    new_k, new_v = [], []
    pend_p = pend_s = None
    for l in range(depth):
        j = l // 2
        g1 = norm1_g[l][None]
        if l % 2 == 0:
            w_bf16 = w_qkv[j].astype(BF16)
            wvt = w_bf16[:, 2 * D_MODEL:].T
            wo_bf16 = w_o[j].astype(BF16)
            qg = jnp.tile(q_norm_g[j], N_HEADS)[None]
            kg = jnp.tile(k_norm_g[j], N_HEADS)[None]
            scale = HEAD_DIM ** -0.5
            qp, kp, vp, xp = _qkv(xp, mods, l, None, g1, w_bf16, None, qg, kg, ones_bd, F32,
                                  scale, pend_p)
            new_k.append(kp.reshape(bp, lp, N_HEADS, HEAD_DIM))
            new_v.append(vp.reshape(bp, lp, N_HEADS, HEAD_DIM))
            xp = _ctx_attention(xp, mods, l, qp, kp, vp, wo_bf16, lp)
            qs, ks, vts, xs = _qkv(xs, mods, l, s_tiles, g1, w_bf16, wvt, qg, kg, ones_bd,
                                   BF16, scale * LOG2_E, pend_s)
            kctx = cache_k[:, j].reshape(bs, -1, D_MODEL).astype(BF16)
            vctx_t = jnp.swapaxes(cache_v[:, j].reshape(bs, -1, D_MODEL), 1, 2).astype(BF16)
            xs = _neighbourhood_attention(xs, mods, l, qs, ks, vts, kctx, vctx_t,
                                          _na_bias_table(rpb[j]), wo_bf16, bs, ls)
        else:
            wp = w_pool[j].astype(BF16)
            ps = pool_scale[j][None]
            xp = _pool_mixer(xp, mods, l, None, g1, wp, ps, lp, lp, pend_p)
            xs = _pool_mixer(xs, mods, l, s_tiles, g1, wp, ps, TOKEN_TILE, ls, pend_s)

        wr = jnp.zeros((ROUTER_ROWS, D_MODEL), F32)
        wr = wr.at[:N_EXPERT_GROUPS].set(w_router_group[l].T)
        wr = wr.at[N_EXPERT_GROUPS:N_EXPERT_GROUPS + N_EXPERTS].set(w_router_expert[l].T)
        wr_hi = wr.astype(BF16)
        wr_lo = (wr - wr_hi.astype(F32)).astype(BF16)
        br = jnp.zeros((ROUTER_ROWS, 1), F32)
        br = br.at[:N_EXPERT_GROUPS, 0].set(b_router_group[l])
        br = br.at[N_EXPERT_GROUPS:N_EXPERT_GROUPS + N_EXPERTS, 0].set(b_router_expert[l])
        router_w = (wr_hi, wr_lo, br)
        g2 = norm2_g[l][None]
        pend_p = _moe(xp, mods, l, None, g2, router_w, expert_w)
        pend_s = _moe(xs, mods, l, s_tiles, g2, router_w, expert_w)

    xp = _unsort_residual(xp, mods, None, pend_p)
    xs = _unsort_residual(xs, mods, s_tiles, pend_s)
    return (xp.reshape(bp, lp, D_MODEL), xs.reshape(bs, ls, D_MODEL),
            jnp.stack(new_k, axis=1), jnp.stack(new_v, axis=1))
```

```python
import functools

import jax
import jax.numpy as jnp
import numpy as np
from jax import lax
from jax.experimental import pallas as pl
from jax.experimental.pallas import tpu as pltpu

D_MODEL = 1024
N_HEADS = 16
HEAD_DIM = D_MODEL // N_HEADS
GRID_W = 64
WIN_ROWS = 8
WIN_COLS = 16
POOL_SIZES = (2, 4, 8, 16)
POOL_GROUP_DIM = D_MODEL // len(POOL_SIZES)
POOL_HALO = 8
N_EXPERT_GROUPS = 4
EXPERTS_PER_GROUP = 4
N_EXPERTS = N_EXPERT_GROUPS * EXPERTS_PER_GROUP
N_PAIRS = 6
N_BUCKETS = N_EXPERT_GROUPS * N_PAIRS
D_FF = D_MODEL // 2
EPS = 1e-6
NEG = -1e30

LANES = 128
MXU_DIM = 256
TOKEN_TILE = 512
MOE_TILE = 256
ROUTER_ROWS = 32
COND_ROWS = 16
VMEM_LIMIT = 56 * 1024 * 1024

NA_ROWS = 4
NA_TOKENS = NA_ROWS * GRID_W
NA_SLOTS = 3 * NA_ROWS
BIAS_NONE = 2 * WIN_ROWS - 1
ONES_ROWS = 16
LOG2_E = 1.4426950408889634
ROW_W = D_MODEL + LANES
GATHER_SLOTS = 3

BF16 = jnp.bfloat16
F32 = jnp.float32


def _params(n_grid_dims, vmem=VMEM_LIMIT):
    return pltpu.CompilerParams(
        dimension_semantics=("arbitrary",) * n_grid_dims, vmem_limit_bytes=vmem)


def _rms_modulate(x, g, shift, scale):
    y = x * lax.rsqrt(jnp.mean(x * x, axis=-1, keepdims=True) + EPS)
    return (y * g) * (1.0 + scale) + shift


def _dot_nt(a, b):
    return lax.dot_general(a, b, (((1,), (1,)), ((), ())), preferred_element_type=F32)


def _mod_kernel(cond_ref, w_ref, b_ref, o_ref):
    c = cond_ref[...]
    s = c / (1.0 + jnp.exp(-c))
    o_ref[...] = jnp.dot(s, w_ref[...], preferred_element_type=F32,
                         precision=lax.Precision.HIGHEST) + b_ref[...]


def _modulation(cond, w_mod, b_mod):
    depth = w_mod.shape[0]
    tn = 1536
    out = pl.pallas_call(
        _mod_kernel,
        grid=(depth, 6 * D_MODEL // tn),
        in_specs=[
            pl.BlockSpec((COND_ROWS, D_MODEL), lambda l, n: (0, 0)),
            pl.BlockSpec((None, D_MODEL, tn), lambda l, n: (l, 0, n)),
            pl.BlockSpec((None, 1, tn), lambda l, n: (l, 0, n)),
        ],
        out_specs=pl.BlockSpec((None, COND_ROWS, tn), lambda l, n: (l, 0, n)),
        out_shape=jax.ShapeDtypeStruct((depth, COND_ROWS, 6 * D_MODEL), F32),
        compiler_params=_params(2),
        name="adaln_modulation",
    )(cond, w_mod, b_mod.reshape(depth, 1, 6 * D_MODEL))
    return out.reshape(depth, COND_ROWS, 6, D_MODEL)


def _mod_spec(layer, tiles_per_cond_row):
    if tiles_per_cond_row is None:
        return pl.BlockSpec((None, None, 6, D_MODEL), lambda i, *_: (layer, 0, 0, 0))
    return pl.BlockSpec((None, None, 6, D_MODEL),
                        lambda i, *_: (layer, 1 + i // tiles_per_cond_row, 0, 0))


def _wait_rows(hbm, buf, sem):
    pltpu.make_async_copy(hbm.at[pl.ds(0, buf.shape[0])], buf, sem).wait()


def _gathered_rows(idx_ref, y_hbm, ybuf, sems, index_of):
    i = pl.program_id(0)
    last = pl.num_programs(0) - 1
    slot = i % GATHER_SLOTS

    def fetch(step, s):
        for r in range(ybuf.shape[1]):
            pltpu.make_async_copy(y_hbm.at[pl.ds(idx_ref[index_of(step, r)], 1)],
                                  ybuf.at[s].at[pl.ds(r, 1)], sems.at[s]).start()

    @pl.when(i == 0)
    def _():
        fetch(0, 0)
        fetch(jnp.minimum(1, last), 1)

    _wait_rows(y_hbm, ybuf.at[slot], sems.at[slot])
    y = ybuf[slot]
    ahead = (i + 2) % GATHER_SLOTS
    fetch(jnp.minimum(i + 2, last), ahead)

    def drain():
        @pl.when(i == last)
        def _():
            for s in (ahead, (i + 1) % GATHER_SLOTS):
                _wait_rows(y_hbm, ybuf.at[s], sems.at[s])

    return y, drain


def _pending_args(pending, mods, tiles_per_cond_row):
    dest, y_sorted, moe_layer = pending
    return ((dest,), (mods, y_sorted),
            [_mod_spec(moe_layer, tiles_per_cond_row), pl.BlockSpec(memory_space=pl.ANY)])


def _head_rms(t, ones_bd, gain):
    sq = (t * t).astype(BF16)
    parts = [jnp.dot(sq[:, c:c + MXU_DIM], ones_bd, preferred_element_type=F32)
             for c in range(0, D_MODEL, MXU_DIM)]
    mean = jnp.concatenate(parts, axis=-1) / HEAD_DIM
    return t * lax.rsqrt(mean + EPS) * gain


def _qkv_kernel(*refs, q_scale, v_transposed, pending):
    if pending:
        (dest_ref, x_ref, mod_ref, g_ref, wqk_ref, wv_ref, qg_ref, kg_ref, bd_ref,
         pmod_ref, y_hbm, q_ref, k_ref, v_ref, xnew_ref, ybuf, sems) = refs
        y, drain = _gathered_rows(dest_ref, y_hbm, ybuf, sems,
                                  lambda step, r: step * TOKEN_TILE + r)
        x = x_ref[...] + pmod_ref[...][5:6] * y
        xnew_ref[...] = x
    else:
        (x_ref, mod_ref, g_ref, wqk_ref, wv_ref, qg_ref, kg_ref, bd_ref,
         q_ref, k_ref, v_ref) = refs
        x, drain = x_ref[...], lambda: None
    m = mod_ref[...]
    h = _rms_modulate(x, g_ref[...], m[0:1], m[1:2]).astype(BF16)
    qk = jnp.dot(h, wqk_ref[...], preferred_element_type=F32)
    bd = bd_ref[...]
    q = _head_rms(qk[:, :D_MODEL], bd, qg_ref[...])
    k = _head_rms(qk[:, D_MODEL:], bd, kg_ref[...])
    q_ref[...] = (q * q_scale).astype(q_ref.dtype)
    k_ref[...] = k.astype(k_ref.dtype)
    if v_transposed:
        vt = _dot_nt(wv_ref[...], h).astype(v_ref.dtype)
        for j in range(TOKEN_TILE // NA_TOKENS):
            v_ref[j] = vt[:, j * NA_TOKENS:(j + 1) * NA_TOKENS]
    else:
        v_ref[...] = jnp.dot(h, wv_ref[...], preferred_element_type=F32).astype(v_ref.dtype)
    drain()


def _qkv(x, mods, layer, tiles_per_cond_row, g, w_bf16, wvt_bf16, qg, kg, bd, kv_dtype,
         q_scale, pending):
    n = x.shape[0]
    row = lambda i, *_: (i, 0)
    const = lambda i, *_: (0, 0)
    tok = pl.BlockSpec((TOKEN_TILE, D_MODEL), row)
    vec = pl.BlockSpec((1, D_MODEL), const)
    if wvt_bf16 is not None:
        per_tile = TOKEN_TILE // NA_TOKENS
        v_spec = pl.BlockSpec((per_tile, D_MODEL, NA_TOKENS), lambda i, *_: (i, 0, 0))
        v_shape = jax.ShapeDtypeStruct((n // NA_TOKENS, D_MODEL, NA_TOKENS), kv_dtype)
        wv, wv_spec = wvt_bf16, pl.BlockSpec((D_MODEL, D_MODEL), const)
    else:
        v_spec, v_shape = tok, jax.ShapeDtypeStruct((n, D_MODEL), kv_dtype)
        wv, wv_spec = w_bf16, pl.BlockSpec((D_MODEL, D_MODEL), lambda i, *_: (0, 2))
    in_specs = [tok, _mod_spec(layer, tiles_per_cond_row), vec,
                pl.BlockSpec((D_MODEL, 2 * D_MODEL), const), wv_spec, vec, vec,
                pl.BlockSpec((MXU_DIM, MXU_DIM), const)]
    out_specs = [tok, tok, v_spec]
    out_shape = [jax.ShapeDtypeStruct((n, D_MODEL), BF16),
                 jax.ShapeDtypeStruct((n, D_MODEL), kv_dtype), v_shape]
    prefetch, extra, scratch = (), (), []
    if pending is not None:
        prefetch, extra, extra_specs = _pending_args(pending, mods, tiles_per_cond_row)
        in_specs += extra_specs
        out_specs.append(tok)
        out_shape.append(jax.ShapeDtypeStruct((n, D_MODEL), F32))
        scratch = [pltpu.VMEM((GATHER_SLOTS, TOKEN_TILE, D_MODEL), F32),
                   pltpu.SemaphoreType.DMA((GATHER_SLOTS,))]
    outs = pl.pallas_call(
        functools.partial(_qkv_kernel, q_scale=q_scale, v_transposed=wvt_bf16 is not None,
                          pending=pending is not None),
        grid_spec=pltpu.PrefetchScalarGridSpec(
            num_scalar_prefetch=len(prefetch), grid=(n // TOKEN_TILE,),
            in_specs=in_specs, out_specs=out_specs, scratch_shapes=scratch),
        out_shape=out_shape,
        compiler_params=_params(1),
        name="qkv_proj",
    )(*prefetch, x, mods, g, w_bf16, wv, qg, kg, bd, *extra)
    return (*outs[:3], outs[3] if pending is not None else x)


def _pair_attention(q2, score_fn, value_fn):
    lane = lax.broadcasted_iota(jnp.int32, q2.shape, 1)
    first = lane < HEAD_DIM
    outs = []
    for keep in (first, jnp.logical_not(first)):
        s = score_fn(jnp.where(keep, q2, jnp.zeros_like(q2)))
        m = functools.reduce(jnp.maximum, [jnp.max(b, axis=-1, keepdims=True) for b in s])
        p = [jnp.exp(b - m) for b in s]
        l = functools.reduce(jnp.add, [jnp.sum(b, axis=-1, keepdims=True) for b in p])
        outs.append(value_fn([b.astype(BF16) for b in p]) / l)
    return jnp.where(first, outs[0], outs[1])


def _ctx_attn_kernel(x_ref, mod_ref, q_ref, k_ref, v_ref, wo_ref, o_ref, att_ref):
    for hp in range(N_HEADS // 2):
        cols = slice(hp * LANES, (hp + 1) * LANES)
        k2 = k_ref[:, cols].astype(BF16)
        v2 = v_ref[:, cols].astype(BF16)
        att = _pair_attention(
            q_ref[:, cols],
            lambda q: [_dot_nt(q, k2)],
            lambda p: jnp.dot(p[0], v2, preferred_element_type=F32))
        att_ref[:, cols] = att.astype(BF16)
    y = jnp.dot(att_ref[...], wo_ref[...], preferred_element_type=F32)
    o_ref[...] = x_ref[...] + mod_ref[...][2:3] * y


def _ctx_attention(x, mods, layer, q, k, v, wo_bf16, seq):
    n = x.shape[0]
    row = lambda i: (i, 0)
    tok = pl.BlockSpec((seq, D_MODEL), row)
    return pl.pallas_call(
        _ctx_attn_kernel,
        grid=(n // seq,),
        in_specs=[tok, _mod_spec(layer, None), tok, tok, tok,
                  pl.BlockSpec((D_MODEL, D_MODEL), lambda i: (0, 0))],
        out_specs=tok,
        out_shape=jax.ShapeDtypeStruct((n, D_MODEL), F32),
        scratch_shapes=[pltpu.VMEM((seq, D_MODEL), BF16)],
        compiler_params=_params(1),
        name="context_attention",
    )(x, mods, q, k, v, wo_bf16)


def _na_kernel(x_ref, mod_ref, q_ref, kp_ref, kc_ref, kn_ref, vp_ref, vc_ref, vn_ref,
               kctx_ref, vctx_ref, bias_ref, wo_ref, o_ref, att_ref, *, n_grid_rows):
    rb = pl.program_id(1)
    entry = []
    for s in range(NA_SLOTS):
        kr = (rb - 1) * NA_ROWS + s
        per_row = []
        for i in range(NA_ROWS):
            r = rb * NA_ROWS + i
            rs = jnp.clip(r - WIN_ROWS // 2, 0, n_grid_rows - WIN_ROWS)
            in_band = jnp.logical_and(kr >= rs, kr < rs + WIN_ROWS)
            per_row.append(jnp.where(in_band, s - i + (WIN_ROWS - 1 - NA_ROWS), BIAS_NONE))
        entry.append(per_row)

    lane = lax.broadcasted_iota(jnp.int32, (GRID_W, LANES), 1)
    left = lane < GRID_W
    qlane = lax.broadcasted_iota(jnp.int32, (NA_TOKENS, LANES), 1)
    k_blocks = (kp_ref, kc_ref, kn_ref)
    v_blocks = (vp_ref, vc_ref, vn_ref)
    for hp in range(N_HEADS // 2):
        cols = slice(hp * LANES, (hp + 1) * LANES)
        q2 = q_ref[:, cols]
        halves = []
        for sub in range(2):
            head = 2 * hp + sub
            keep = (qlane < HEAD_DIM) if sub == 0 else (qlane >= HEAD_DIM)
            qm = jnp.where(keep, q2, jnp.zeros_like(q2))
            s_blocks = []
            for j in range(3):
                st = _dot_nt(k_blocks[j][:, cols], qm)
                rows = []
                for sr in range(NA_ROWS):
                    s = j * NA_ROWS + sr
                    bias = jnp.concatenate(
                        [jnp.where(left, bias_ref[head, entry[s][2 * t]],
                                   bias_ref[head, entry[s][2 * t + 1]])
                         for t in range(NA_ROWS // 2)], axis=-1)
                    rows.append(st[sr * GRID_W:(sr + 1) * GRID_W, :] + bias)
                s_blocks.append(jnp.concatenate(rows, axis=0))
            s_blocks.append(_dot_nt(kctx_ref[:, cols], qm))
            m = functools.reduce(jnp.maximum,
                                 [jnp.max(b, axis=0, keepdims=True) for b in s_blocks])
            p = [jnp.exp2(b - m).astype(BF16) for b in s_blocks]
            vrows = slice(hp * LANES + sub * HEAD_DIM, hp * LANES + (sub + 1) * HEAD_DIM)
            vt = [jnp.concatenate([vb[vrows, :], jnp.ones((ONES_ROWS, vb.shape[1]), BF16)],
                                  axis=0) for vb in v_blocks + (vctx_ref,)]
            ot = functools.reduce(jnp.add, [
                jnp.dot(vt[j], p[j], preferred_element_type=F32) for j in range(4)])
            halves.append(ot[:HEAD_DIM] / ot[HEAD_DIM:HEAD_DIM + 1])
        att_ref[:, cols] = jnp.concatenate(halves, axis=0).T.astype(BF16)
    y = jnp.dot(att_ref[...], wo_ref[...], preferred_element_type=F32)
    o_ref[...] = x_ref[...] + mod_ref[...][2:3] * y


def _neighbourhood_attention(x, mods, layer, q, k, vt, kctx, vctx_t, bias, wo_bf16, batch, seq):
    n_grid_rows = seq // GRID_W
    nrb = n_grid_rows // NA_ROWS
    cur = lambda b, r: (b * nrb + r, 0)
    prev = lambda b, r: (b * nrb + jnp.maximum(r - 1, 0), 0)
    nxt = lambda b, r: (b * nrb + jnp.minimum(r + 1, nrb - 1), 0)
    blk = lambda f: pl.BlockSpec((NA_TOKENS, D_MODEL), f)
    blk_t = lambda f: pl.BlockSpec((None, D_MODEL, NA_TOKENS),
                                   lambda b, r: (f(b, r)[0], 0, 0))
    n_ctx = kctx.shape[1]
    return pl.pallas_call(
        functools.partial(_na_kernel, n_grid_rows=n_grid_rows),
        grid=(batch, nrb),
        in_specs=[blk(cur),
                  pl.BlockSpec((None, None, 6, D_MODEL), lambda b, r: (layer, 1 + b, 0, 0)),
                  blk(cur), blk(prev), blk(cur), blk(nxt),
                  blk_t(prev), blk_t(cur), blk_t(nxt),
                  pl.BlockSpec((None, n_ctx, D_MODEL), lambda b, r: (b, 0, 0)),
                  pl.BlockSpec((None, D_MODEL, n_ctx), lambda b, r: (b, 0, 0)),
                  pl.BlockSpec(bias.shape, lambda b, r: (0, 0, 0, 0)),
                  pl.BlockSpec((D_MODEL, D_MODEL), lambda b, r: (0, 0))],
        out_specs=blk(cur),
        out_shape=jax.ShapeDtypeStruct(x.shape, F32),
        scratch_shapes=[pltpu.VMEM((NA_TOKENS, D_MODEL), BF16)],
        compiler_params=_params(2),
        name="neighbourhood_attention",
    )(x, mods, q, k, k, k, vt, vt, vt, kctx, vctx_t, bias, wo_bf16)


def _na_bias_table(rpb_layer):
    qc = np.arange(GRID_W)[None, :]
    kc = np.arange(GRID_W)[:, None]
    start = np.clip(qc - WIN_COLS // 2, 0, GRID_W - WIN_COLS)
    valid = (kc >= start) & (kc < start + WIN_COLS)
    off = kc - qc + WIN_COLS - 1
    select = ((off[None] == np.arange(2 * WIN_COLS - 1)[:, None, None]) & valid[None])
    m = jnp.einsum('hro,okq->hrkq', rpb_layer * LOG2_E, jnp.asarray(select, F32),
                   precision=lax.Precision.HIGHEST)
    m = m + jnp.asarray(np.where(valid, 0.0, NEG), F32)
    m = jnp.concatenate([m, jnp.full_like(m[:, :1], NEG)], axis=1)
    return jnp.concatenate([m, m], axis=-1)


def _pool_kernel(*refs, tile, seq, n_tokens, pending):
    if pending:
        (dest_ref, x_ref, xp_ref, xn_ref, mod_ref, g_ref, w_ref, ps_ref, pmod_ref, y_hbm,
         o_ref, h_ref, s_ref, ybuf, sems) = refs

        def index_of(step, r):
            pos = step * tile + (r - POOL_HALO)
            inside = POOL_HALO <= r < POOL_HALO + tile
            return pos if inside else jnp.clip(pos, 0, n_tokens - 1)

        y, drain = _gathered_rows(dest_ref, y_hbm, ybuf, sems, index_of)
        gate = pmod_ref[...][5:6]
        x_prev = xp_ref[...] + gate * y[:POOL_HALO]
        x_cur = x_ref[...] + gate * y[POOL_HALO:POOL_HALO + tile]
        x_next = xn_ref[...] + gate * y[POOL_HALO + tile:]
    else:
        (x_ref, xp_ref, xn_ref, mod_ref, g_ref, w_ref, ps_ref, o_ref, h_ref, s_ref) = refs
        x_prev, x_cur, x_next = xp_ref[...], x_ref[...], xn_ref[...]
        drain = lambda: None
    i = pl.program_id(0)
    tiles_per_seq = seq // tile
    t_in_seq = i % tiles_per_seq
    m = mod_ref[...]
    g = g_ref[...]
    h_cur = _rms_modulate(x_cur, g, m[0:1], m[1:2])
    h_prev = _rms_modulate(x_prev, g, m[0:1], m[1:2])
    h_next = _rms_modulate(x_next, g, m[0:1], m[1:2])
    h_ref[0:POOL_HALO, :] = jnp.where(t_in_seq > 0, h_prev, 0.0)
    h_ref[POOL_HALO:POOL_HALO + tile, :] = h_cur
    h_ref[POOL_HALO + tile:, :] = jnp.where(t_in_seq < tiles_per_seq - 1, h_next, 0.0)

    pos = t_in_seq * tile + lax.broadcasted_iota(jnp.int32, (tile, 1), 0)
    ys = []
    for grp, w in enumerate(POOL_SIZES):
        cols = slice(grp * POOL_GROUP_DIM, (grp + 1) * POOL_GROUP_DIM)
        cur, span, valid = h_ref, 1, tile + 2 * POOL_HALO
        while span < w // 2:
            s_ref[0:valid - span, cols] = cur[0:valid - span, cols] + cur[span:valid, cols]
            cur, valid, span = s_ref, valid - span, 2 * span
        total = (cur[POOL_HALO - span:POOL_HALO - span + tile, cols]
                 + cur[POOL_HALO:POOL_HALO + tile, cols])
        lo = jnp.clip(pos - w // 2, 0, seq)
        hi = jnp.clip(pos - w // 2 + w, 0, seq)
        pooled = total / (hi - lo).astype(F32)
        diff = (pooled - h_cur[:, cols]).astype(BF16)
        ys.append(jnp.dot(diff, w_ref[grp], preferred_element_type=F32))
    y_mix = jnp.concatenate(ys, axis=-1) * ps_ref[...]
    o_ref[...] = x_cur + m[2:3] * y_mix
    drain()


def _pool_mixer(x, mods, layer, tiles_per_cond_row, g, w_pool_bf16, pool_scale, tile, seq,
                pending):
    n = x.shape[0]
    hb = tile // POOL_HALO
    last = n // POOL_HALO - 1
    const = lambda i, *_: (0, 0)
    in_specs = [pl.BlockSpec((tile, D_MODEL), lambda i, *_: (i, 0)),
                pl.BlockSpec((POOL_HALO, D_MODEL),
                             lambda i, *_: (jnp.maximum(i * hb - 1, 0), 0)),
                pl.BlockSpec((POOL_HALO, D_MODEL),
                             lambda i, *_: (jnp.minimum((i + 1) * hb, last), 0)),
                _mod_spec(layer, tiles_per_cond_row),
                pl.BlockSpec((1, D_MODEL), const),
                pl.BlockSpec(w_pool_bf16.shape, lambda i, *_: (0, 0, 0)),
                pl.BlockSpec((1, D_MODEL), const)]
    scratch = [pltpu.VMEM((tile + 2 * POOL_HALO, D_MODEL), F32)] * 2
    prefetch, extra = (), ()
    if pending is not None:
        prefetch, extra, extra_specs = _pending_args(pending, mods, tiles_per_cond_row)
        in_specs += extra_specs
        scratch += [pltpu.VMEM((GATHER_SLOTS, tile + 2 * POOL_HALO, D_MODEL), F32),
                    pltpu.SemaphoreType.DMA((GATHER_SLOTS,))]
    return pl.pallas_call(
        functools.partial(_pool_kernel, tile=tile, seq=seq, n_tokens=n,
                          pending=pending is not None),
        grid_spec=pltpu.PrefetchScalarGridSpec(
            num_scalar_prefetch=len(prefetch), grid=(n // tile,),
            in_specs=in_specs,
            out_specs=pl.BlockSpec((tile, D_MODEL), lambda i, *_: (i, 0)),
            scratch_shapes=scratch),
        out_shape=jax.ShapeDtypeStruct(x.shape, F32),
        compiler_params=_params(1),
        name="pool_mixer",
    )(*prefetch, x, x, x, mods, g, w_pool_bf16, pool_scale, *extra)


def _first_argmax(vals):
    best = functools.reduce(jnp.maximum, vals)
    idx = jnp.full(best.shape, len(vals) - 1, jnp.int32)
    for j in range(len(vals) - 2, -1, -1):
        idx = jnp.where(vals[j] == best, j, idx)
    return best, idx


def _router_kernel(x_ref, mod_ref, g_ref, wr_hi_ref, wr_lo_ref, br_ref, aux_ref):
    m = mod_ref[...]
    h = _rms_modulate(x_ref[...], g_ref[...], m[3:4], m[4:5])
    h_hi = h.astype(BF16)
    h_lo = (h - h_hi.astype(F32)).astype(BF16)
    w_hi = wr_hi_ref[...]
    both = _dot_nt(jnp.concatenate([w_hi, wr_lo_ref[...]], axis=0), h_hi)
    logits = (both[:ROUTER_ROWS] + _dot_nt(w_hi, h_lo) + both[ROUTER_ROWS:]) + br_ref[...]
    row = lambda r: logits[r:r + 1, :]
    grp = [row(j) for j in range(N_EXPERT_GROUPS)]
    g_max, g_sel = _first_argmax(grp)
    g_w = 1.0 / functools.reduce(jnp.add, [jnp.exp(v - g_max) for v in grp])
    e_in = []
    for j in range(EXPERTS_PER_GROUP):
        v = row(N_EXPERT_GROUPS + (N_EXPERT_GROUPS - 1) * EXPERTS_PER_GROUP + j)
        for gi in range(N_EXPERT_GROUPS - 2, -1, -1):
            v = jnp.where(g_sel == gi, row(N_EXPERT_GROUPS + gi * EXPERTS_PER_GROUP + j), v)
        e_in.append(v)
    v1, i1 = _first_argmax(e_in)
    v2, i2 = _first_argmax([jnp.where(i1 == j, -jnp.inf, e_in[j])
                            for j in range(EXPERTS_PER_GROUP)])
    t = jnp.exp(v2 - v1)
    w1 = g_w / (1.0 + t)
    w2 = g_w * t / (1.0 + t)
    lo = jnp.minimum(i1, i2)
    hi = jnp.maximum(i1, i2)
    w_lo = jnp.where(i1 < i2, w1, w2)
    w_hi = jnp.where(i1 < i2, w2, w1)
    pair = jnp.where(lo == 0, hi - 1, jnp.where(lo == 1, hi + 1, 5))
    bucket = (g_sel * N_PAIRS + pair).astype(F32)
    pad = jnp.zeros((5, bucket.shape[1]), F32)
    aux_ref[...] = jnp.concatenate([bucket, w_lo, w_hi, pad], axis=0)


def _router(x, mods, layer, tiles_per_cond_row, g, wr_hi, wr_lo, br):
    n = x.shape[0]
    const = lambda i: (0, 0)
    tok = pl.BlockSpec((TOKEN_TILE, D_MODEL), lambda i: (i, 0))
    return pl.pallas_call(
        _router_kernel,
        grid=(n // TOKEN_TILE,),
        in_specs=[tok, _mod_spec(layer, tiles_per_cond_row),
                  pl.BlockSpec((1, D_MODEL), const),
                  pl.BlockSpec((ROUTER_ROWS, D_MODEL), const),
                  pl.BlockSpec((ROUTER_ROWS, D_MODEL), const),
                  pl.BlockSpec((ROUTER_ROWS, 1), const)],
        out_specs=pl.BlockSpec((8, TOKEN_TILE), lambda i: (0, i)),
        out_shape=jax.ShapeDtypeStruct((8, n), F32),
        compiler_params=_params(1),
        name="moe_router",
    )(x, mods, g, wr_hi, wr_lo, br)


def _dispatch_kernel(dest_ref, zero_at_ref, x_ref, mod_ref, g_ref, cw_ref, hs_hbm,
                     rowbuf, zbuf, sems, zsem):
    i = pl.program_id(0)
    last = pl.num_programs(0) - 1
    slot = i % 2

    @pl.when(i == 0)
    def _():
        zbuf[...] = jnp.zeros_like(zbuf)

        def clear(row0):
            return pltpu.make_async_copy(
                zbuf, hs_hbm.at[pl.ds(pl.multiple_of(row0, MOE_TILE), MOE_TILE)], zsem)

        n_tiles = hs_hbm.shape[0] // MOE_TILE
        n_used = zero_at_ref[N_BUCKETS]
        for b in range(N_BUCKETS):
            @pl.when(zero_at_ref[b] >= 0)
            def _():
                clear(zero_at_ref[b]).start()
        lax.fori_loop(n_used, n_tiles, lambda t, c: (clear(t * MOE_TILE).start(), c)[1], 0)
        for b in range(N_BUCKETS):
            @pl.when(zero_at_ref[b] >= 0)
            def _():
                clear(zero_at_ref[b]).wait()
        lax.fori_loop(n_used, n_tiles, lambda t, c: (clear(t * MOE_TILE).wait(), c)[1], 0)

    @pl.when(i >= 2)
    def _():
        _wait_rows(hs_hbm, rowbuf.at[slot], sems.at[slot])

    m = mod_ref[...]
    rowbuf[slot, :, :D_MODEL] = _rms_modulate(x_ref[...], g_ref[...], m[3:4], m[4:5])
    rowbuf[slot, :, D_MODEL:] = cw_ref[...]
    for r in range(TOKEN_TILE):
        pltpu.make_async_copy(rowbuf.at[slot].at[pl.ds(r, 1)],
                              hs_hbm.at[pl.ds(dest_ref[i * TOKEN_TILE + r], 1)],
                              sems.at[slot]).start(priority=r % 2)

    @pl.when(i == last)
    def _():
        @pl.when(i >= 1)
        def _():
            _wait_rows(hs_hbm, rowbuf.at[1 - slot], sems.at[1 - slot])
        _wait_rows(hs_hbm, rowbuf.at[slot], sems.at[slot])


def _dispatch(x, mods, layer, tiles_per_cond_row, g, dest, zero_at, cw_nat, n_pad):
    n = x.shape[0]
    tok = lambda w: pl.BlockSpec((TOKEN_TILE, w), lambda i, *_: (i, 0))
    return pl.pallas_call(
        _dispatch_kernel,
        grid_spec=pltpu.PrefetchScalarGridSpec(
            num_scalar_prefetch=2,
            grid=(n // TOKEN_TILE,),
            in_specs=[tok(D_MODEL), _mod_spec(layer, tiles_per_cond_row),
                      pl.BlockSpec((1, D_MODEL), lambda i, *_: (0, 0)), tok(LANES)],
            out_specs=pl.BlockSpec(memory_space=pl.ANY),
            scratch_shapes=[pltpu.VMEM((2, TOKEN_TILE, ROW_W), F32),
                            pltpu.VMEM((MOE_TILE, ROW_W), F32),
                            pltpu.SemaphoreType.DMA((2,)),
                            pltpu.SemaphoreType.DMA(())]),
        out_shape=jax.ShapeDtypeStruct((n_pad, ROW_W), F32),
        compiler_params=_params(1),
        name="moe_dispatch",
    )(dest, zero_at, x, mods, g, cw_nat)


def _expert_kernel(elo_ref, ehi_ref, nused_ref, hs_ref,
                   wg_lo, wu_lo, wd_lo, wg_hi, wu_hi, wd_hi, o_ref):
    t = pl.program_id(0)

    @pl.when(t < nused_ref[0])
    def _():
        x = hs_ref[:, :D_MODEL].astype(BF16)
        cw = hs_ref[:, D_MODEL:]
        y = None
        for e, (wg, wu, wd) in enumerate(((wg_lo, wu_lo, wd_lo), (wg_hi, wu_hi, wd_hi))):
            a = jnp.dot(x, wg[...], preferred_element_type=F32)
            u = jnp.dot(x, wu[...], preferred_element_type=F32)
            act = (a / (1.0 + jnp.exp(-a))) * u * cw[:, e:e + 1]
            part = jnp.dot(act.astype(BF16), wd[...], preferred_element_type=F32)
            y = part if y is None else y + part
        o_ref[...] = y

    @pl.when(t >= nused_ref[0])
    def _():
        o_ref[...] = jnp.zeros_like(o_ref)


def _experts(hs, e_lo, e_hi, n_used, layer, wg, wu, wd):
    n_pad = hs.shape[0]
    row_in = lambda t, elo, ehi, nu: (jnp.minimum(t, nu[0] - 1), 0)
    row_out = lambda t, *_: (t, 0)
    w_lo = lambda t, elo, ehi, nu: (layer, elo[t], 0, 0)
    w_hi = lambda t, elo, ehi, nu: (layer, ehi[t], 0, 0)
    up = lambda f: pl.BlockSpec((None, None, D_MODEL, D_FF), f)
    down = lambda f: pl.BlockSpec((None, None, D_FF, D_MODEL), f)
    return pl.pallas_call(
        _expert_kernel,
        grid_spec=pltpu.PrefetchScalarGridSpec(
            num_scalar_prefetch=3,
            grid=(n_pad // MOE_TILE,),
            in_specs=[pl.BlockSpec((MOE_TILE, ROW_W), row_in),
                      up(w_lo), up(w_lo), down(w_lo), up(w_hi), up(w_hi), down(w_hi)],
            out_specs=pl.BlockSpec((MOE_TILE, D_MODEL), row_out)),
        out_shape=jax.ShapeDtypeStruct((n_pad, D_MODEL), F32),
        compiler_params=_params(1),
        name="moe_experts",
    )(e_lo, e_hi, n_used, hs, wg, wu, wd, wg, wu, wd)


def _unsort_kernel(dest_ref, x_ref, mod_ref, y_hbm, o_ref, ybuf, sems):
    y, drain = _gathered_rows(dest_ref, y_hbm, ybuf, sems,
                              lambda step, r: step * TOKEN_TILE + r)
    o_ref[...] = x_ref[...] + mod_ref[...][5:6] * y
    drain()


def _unsort_residual(x, mods, tiles_per_cond_row, pending):
    n = x.shape[0]
    tok = pl.BlockSpec((TOKEN_TILE, D_MODEL), lambda i, *_: (i, 0))
    prefetch, extra, extra_specs = _pending_args(pending, mods, tiles_per_cond_row)
    return pl.pallas_call(
        _unsort_kernel,
        grid_spec=pltpu.PrefetchScalarGridSpec(
            num_scalar_prefetch=len(prefetch),
            grid=(n // TOKEN_TILE,),
            in_specs=[tok] + extra_specs,
            out_specs=tok,
            scratch_shapes=[pltpu.VMEM((GATHER_SLOTS, TOKEN_TILE, D_MODEL), F32),
                            pltpu.SemaphoreType.DMA((GATHER_SLOTS,))]),
        out_shape=jax.ShapeDtypeStruct(x.shape, F32),
        compiler_params=_params(1),
        name="moe_unsort_residual",
    )(*prefetch, x, *extra)


_PAIR_LO = (0, 0, 0, 1, 1, 2)
_PAIR_HI = (1, 2, 3, 2, 3, 3)


def _dispatch_plan(aux, n):
    n_tiles = n // MOE_TILE + N_BUCKETS
    bucket = aux[0].astype(jnp.int32)
    onehot = (bucket[:, None] == jnp.arange(N_BUCKETS)[None, :]).astype(jnp.int32)
    csum = jnp.cumsum(onehot, axis=0)
    counts = csum[-1]
    tiles = (counts + MOE_TILE - 1) // MOE_TILE
    tile_end = jnp.cumsum(tiles)
    tile_start = tile_end - tiles
    n_used = tile_end[-1]
    slot0 = tile_start * MOE_TILE - 1
    dest = jnp.sum(onehot * (csum + slot0[None, :]), axis=1).astype(jnp.int32)
    zero_at = jnp.where(tiles > 0, (tile_end - 1) * MOE_TILE, -1)
    zero_at = jnp.concatenate([zero_at, n_used[None]]).astype(jnp.int32)
    t = jnp.minimum(jnp.arange(n_tiles), n_used - 1)
    tile_bucket = jnp.sum((t[:, None] >= tile_end[None, :]).astype(jnp.int32), axis=1)
    grp = tile_bucket // N_PAIRS
    pair = tile_bucket % N_PAIRS
    e_lo = grp * EXPERTS_PER_GROUP + jnp.asarray(_PAIR_LO, jnp.int32)[pair]
    e_hi = grp * EXPERTS_PER_GROUP + jnp.asarray(_PAIR_HI, jnp.int32)[pair]
    cw_nat = jnp.pad(aux[1:3].T, ((0, 0), (0, LANES - 2)))
    return (dest, zero_at, e_lo.astype(jnp.int32), e_hi.astype(jnp.int32),
            n_used.astype(jnp.int32).reshape(1), cw_nat, n_tiles * MOE_TILE)


def _moe(x, mods, layer, tiles_per_cond_row, g, router_w, expert_w):
    aux = _router(x, mods, layer, tiles_per_cond_row, g, *router_w)
    dest, zero_at, e_lo, e_hi, n_used, cw_nat, n_pad = _dispatch_plan(aux, x.shape[0])
    hs = _dispatch(x, mods, layer, tiles_per_cond_row, g, dest, zero_at, cw_nat, n_pad)
    y_sorted = _experts(hs, e_lo, e_hi, n_used, layer, *expert_w)
    return dest, y_sorted, layer


def kernel(x_prompt, x_sample, cache_k, cache_v, c, c_ctx, norm1_g, norm2_g, w_mod, b_mod,
           w_qkv, q_norm_g, k_norm_g, rpb, w_o, w_pool, pool_scale, w_router_group,
           b_router_group, w_router_expert, b_router_expert, w_gate, w_up, w_down):
    bp, lp, _ = x_prompt.shape
    bs, ls, _ = x_sample.shape
    depth = w_mod.shape[0]
    assert bs + 1 <= COND_ROWS and ls % TOKEN_TILE == 0 and (bp * lp) % TOKEN_TILE == 0
    assert lp <= TOKEN_TILE and TOKEN_TILE % lp == 0

    cond = jnp.zeros((COND_ROWS, D_MODEL), F32).at[0].set(c_ctx).at[1:1 + bs].set(c)
    mods = _modulation(cond, w_mod, b_mod)

    xp = x_prompt.reshape(bp * lp, D_MODEL)
    xs = x_sample.reshape(bs * ls, D_MODEL)
    s_tiles = ls // TOKEN_TILE

    head_of = jnp.arange(MXU_DIM) // HEAD_DIM
    ones_bd = (head_of[:, None] == head_of[None, :]).astype(BF16)

    expert_w = (w_gate.astype(BF16), w_up.astype(BF16), w_down.astype(BF16))
    new_k, new_v = [], []
    pend_p = pend_s = None
    for l in range(depth):
        j = l // 2
        g1 = norm1_g[l][None]
        if l % 2 == 0:
            w_bf16 = w_qkv[j].astype(BF16)
            wvt = w_bf16[:, 2 * D_MODEL:].T
            wo_bf16 = w_o[j].astype(BF16)
            qg = jnp.tile(q_norm_g[j], N_HEADS)[None]
            kg = jnp.tile(k_norm_g[j], N_HEADS)[None]
            scale = HEAD_DIM ** -0.5
            qp, kp, vp, xp = _qkv(xp, mods, l, None, g1, w_bf16, None, qg, kg, ones_bd, F32,
                                  scale, pend_p)
            new_k.append(kp.reshape(bp, lp, N_HEADS, HEAD_DIM))
            new_v.append(vp.reshape(bp, lp, N_HEADS, HEAD_DIM))
            xp = _ctx_attention(xp, mods, l, qp, kp, vp, wo_bf16, lp)
            qs, ks, vts, xs = _qkv(xs, mods, l, s_tiles, g1, w_bf16, wvt, qg, kg, ones_bd,
                                   BF16, scale * LOG2_E, pend_s)
            kctx = cache_k[:, j].reshape(bs, -1, D_MODEL).astype(BF16)
            vctx_t = jnp.swapaxes(cache_v[:, j].reshape(bs, -1, D_MODEL), 1, 2).astype(BF16)
            xs = _neighbourhood_attention(xs, mods, l, qs, ks, vts, kctx, vctx_t,
                                          _na_bias_table(rpb[j]), wo_bf16, bs, ls)
        else:
            wp = w_pool[j].astype(BF16)
            ps = pool_scale[j][None]
            xp = _pool_mixer(xp, mods, l, None, g1, wp, ps, lp, lp, pend_p)
            xs = _pool_mixer(xs, mods, l, s_tiles, g1, wp, ps, TOKEN_TILE, ls, pend_s)

        wr = jnp.zeros((ROUTER_ROWS, D_MODEL), F32)
        wr = wr.at[:N_EXPERT_GROUPS].set(w_router_group[l].T)
        wr = wr.at[N_EXPERT_GROUPS:N_EXPERT_GROUPS + N_EXPERTS].set(w_router_expert[l].T)
        wr_hi = wr.astype(BF16)
        wr_lo = (wr - wr_hi.astype(F32)).astype(BF16)
        br = jnp.zeros((ROUTER_ROWS, 1), F32)
        br = br.at[:N_EXPERT_GROUPS, 0].set(b_router_group[l])
        br = br.at[N_EXPERT_GROUPS:N_EXPERT_GROUPS + N_EXPERTS, 0].set(b_router_expert[l])
        router_w = (wr_hi, wr_lo, br)
        g2 = norm2_g[l][None]
        pend_p = _moe(xp, mods, l, None, g2, router_w, expert_w)
        pend_s = _moe(xs, mods, l, s_tiles, g2, router_w, expert_w)

    xp = _unsort_residual(xp, mods, None, pend_p)
    xs = _unsort_residual(xs, mods, s_tiles, pend_s)
    return (xp.reshape(bp, lp, D_MODEL), xs.reshape(bs, ls, D_MODEL),
            jnp.stack(new_k, axis=1), jnp.stack(new_v, axis=1))
```
